```python
import math, functools
import jax, jax.numpy as jnp
from jax import lax
import numpy as np

D_MODEL = 1024
BATCH = 8
SEQ = 2048
DEPTH = 1
DEC_BATCH = 128
DEC_SEQ = 8
PAST_LEN = 16384
PAGE_SIZE = 128

N_META = 16
DN_HEADS = 16
DN_HEAD_K = 64
DN_HEAD_V = 64
DN_QK_DIM = DN_HEADS * DN_HEAD_K
DN_V_DIM = DN_HEADS * DN_HEAD_V
DN_QKV_DIM = 2 * DN_QK_DIM + DN_V_DIM
SSM_HEADS = 16
SSM_HEAD_DIM = 64
SSM_GROUPS = 2
SSM_STATE = 128
D_SSM = SSM_HEADS * SSM_HEAD_DIM
SSM_XBC_DIM = D_SSM + 2 * SSM_GROUPS * SSM_STATE
D_MIX = DN_V_DIM + D_SSM
PROJ_DIM = DN_QKV_DIM + DN_V_DIM + 2 * DN_HEADS + D_SSM + SSM_XBC_DIM + SSM_HEADS
CONV_K = 4
CHUNK = 64
PEER_HEADS = 8
N_KEYS = 128
N_EXPERTS = N_KEYS * N_KEYS
D_KEY = 256
TOP_K = 16
PEER_BLOCK = 128
ALPHA = (2 * DEPTH) ** 0.25
BETA_INIT = (8 * DEPTH) ** -0.25

kernel_name = 'hymba_gdn_ssd_peer_step'


def layer_norm(x, g, b, eps=1e-5):
    xf = x.astype(jnp.float32)
    mu = jnp.mean(xf, axis=-1, keepdims=True)
    var = jnp.mean(jnp.square(xf - mu), axis=-1, keepdims=True)
    return ((xf - mu) * lax.rsqrt(var + eps) * g.astype(jnp.float32) + b.astype(jnp.float32)).astype(x.dtype)


def rms_norm(x, g, eps=1e-6):
    xf = x.astype(jnp.float32)
    return (xf * lax.rsqrt(jnp.mean(jnp.square(xf), -1, keepdims=True) + eps) * g.astype(jnp.float32)).astype(x.dtype)


def l2_normalize(x, eps=1e-6):
    xf = x.astype(jnp.float32)
    return (xf * lax.rsqrt(jnp.sum(jnp.square(xf), -1, keepdims=True) + eps)).astype(x.dtype)


def causal_conv(x, prev, w, b=None):
    k_width, length = w.shape[0], x.shape[1]
    xp = jnp.concatenate([prev.astype(x.dtype), x], axis=1)
    y = sum(w[i] * xp[:, i:i + length] for i in range(k_width))
    if b is not None:
        y = y + b
    return y, xp[:, length:]


def gated_delta_chunked(q, k, v, g, beta, s0, chunk):
    bsz, length, heads, _ = q.shape
    dv = v.shape[-1]
    n = length // chunk

    def to_chunks(a):
        a = a.astype(jnp.float32).reshape(bsz, n, chunk, heads, *a.shape[3:])
        return jnp.moveaxis(a, (1, 3), (0, 2))

    causal = jnp.tril(jnp.ones((chunk, chunk), bool))
    strict = jnp.tril(jnp.ones((chunk, chunk), bool), -1)
    eye = jnp.eye(chunk, dtype=jnp.float32)

    def body(s, inp):
        qc, kc, vc, gc, bc = inp
        cum = jnp.cumsum(gc, axis=-1)
        decay = jnp.exp(jnp.where(causal, cum[..., :, None] - cum[..., None, :], -jnp.inf))
        kk = jnp.einsum('bhid,bhjd->bhij', kc, kc)
        a_mat = jnp.where(strict, bc[..., :, None] * kk * decay, 0.0)
        rhs = jnp.concatenate([bc[..., None] * vc, (bc * jnp.exp(cum))[..., None] * kc], axis=-1)
        sol = lax.linalg.triangular_solve(a_mat + eye, rhs, left_side=True, lower=True, unit_diagonal=True)
        u, w = sol[..., :dv], sol[..., dv:]
        delta = u - jnp.einsum('bhck,bhkv->bhcv', w, s)
        qk = jnp.where(causal, jnp.einsum('bhid,bhjd->bhij', qc, kc) * decay, 0.0)
        o = jnp.exp(cum)[..., None] * jnp.einsum('bhck,bhkv->bhcv', qc, s) + jnp.einsum('bhij,bhjv->bhiv', qk, delta)
        last = cum[..., -1:]
        s_new = jnp.exp(last)[..., None] * s + jnp.einsum('bhck,bhcv->bhkv', jnp.exp(last - cum)[..., None] * kc, delta)
        return s_new, o

    s_fin, o = lax.scan(body, s0.astype(jnp.float32), tuple(to_chunks(a) for a in (q, k, v, g, beta)))
    o = jnp.moveaxis(o, (0, 2), (1, 3)).reshape(bsz, length, heads, dv)
    return o, s_fin


def ssd_chunked(x, dt, b_in, c_in, h0, chunk, a_neg):
    bsz, length, heads, hdim = x.shape
    groups, nst = b_in.shape[2:]
    rep = heads // groups
    n = length // chunk

    def to_chunks(a):
        return jnp.moveaxis(a.astype(jnp.float32).reshape(bsz, n, chunk, *a.shape[2:]), 1, 0)

    causal = jnp.tril(jnp.ones((chunk, chunk), bool))
    a_g = a_neg.astype(jnp.float32).reshape(groups, rep)

    def body(h, inp):
        xc, dtc, bc, cc = inp
        cum = jnp.cumsum(dtc * a_g, axis=1)
        cum_t = jnp.moveaxis(cum, 1, -1)
        lmat = jnp.exp(jnp.where(causal, cum_t[..., :, None] - cum_t[..., None, :], -jnp.inf))
        cb = jnp.einsum('bign,bjgn->bgij', cc, bc)
        xdt = xc * dtc[..., None]
        y_intra = jnp.einsum('bgrij,bjgrp->bigrp', lmat * cb[:, :, None], xdt)
        y_inter = jnp.exp(cum)[..., None] * jnp.einsum('bign,bgrpn->bigrp', cc, h)
        last = cum[:, -1:]
        h_new = jnp.exp(last[:, 0])[..., None, None] * h + jnp.einsum('bjgrp,bjgn->bgrpn', xdt * jnp.exp(last - cum)[..., None], bc)
        return h_new, y_intra + y_inter

    xs = (x.reshape(bsz, length, groups, rep, hdim), dt.reshape(bsz, length, groups, rep), b_in, c_in)
    h_fin, y = lax.scan(body, h0.astype(jnp.float32).reshape(bsz, groups, rep, hdim, nst), tuple(to_chunks(a) for a in xs))
    y = jnp.moveaxis(y, 0, 1).reshape(bsz, length, heads, hdim)
    return y, h_fin.reshape(bsz, heads, hdim, nst)


def run_segments(chunk_fn, arrays, state, n_lead):
    length = arrays[0].shape[1]
    outs = []
    for start, stop in ((0, n_lead), (n_lead, length)):
        if stop > start:
            o, state = chunk_fn(*(a[:, start:stop] for a in arrays), state, math.gcd(stop - start, CHUNK))
            outs.append(o)
    return jnp.concatenate(outs, axis=1), state


def mixer_sublayer(h, dn_conv_prev, dn_s0, ssm_conv_prev, ssm_h0, n_lead,
                   w_in, dn_conv_w, dn_A_log, dn_dt_bias, dn_norm_g,
                   ssm_conv_w, ssm_conv_b, ssm_A_log, ssm_dt_bias, ssm_D, ssm_norm_g, w_out):
    bsz, length, _ = h.shape
    dtype = h.dtype
    sizes = (DN_QKV_DIM, DN_V_DIM, DN_HEADS, DN_HEADS, D_SSM, SSM_XBC_DIM, SSM_HEADS)
    points = [int(p) for p in np.cumsum(sizes)[:-1]]
    proj = jnp.einsum('bld,de->ble', h, w_in)
    dn_qkv, dn_z, dn_b, dn_a, ssm_z, ssm_xbc, ssm_dt = jnp.split(proj, points, axis=-1)

    qkv, dn_conv_new = causal_conv(dn_qkv, dn_conv_prev, dn_conv_w)
    qkv = jax.nn.silu(qkv)
    q, k, v = jnp.split(qkv, [DN_QK_DIM, 2 * DN_QK_DIM], axis=-1)
    q = l2_normalize(q.reshape(bsz, length, DN_HEADS, DN_HEAD_K)) * (DN_HEAD_K ** -0.5)
    k = l2_normalize(k.reshape(bsz, length, DN_HEADS, DN_HEAD_K))
    v = v.reshape(bsz, length, DN_HEADS, DN_HEAD_V)
    beta = jax.nn.sigmoid(dn_b.astype(jnp.float32))
    g = -jnp.exp(dn_A_log.astype(jnp.float32)) * jax.nn.softplus(dn_a.astype(jnp.float32) + dn_dt_bias.astype(jnp.float32))
    o, dn_s = run_segments(gated_delta_chunked, (q, k, v, g, beta), dn_s0, n_lead)
    o = rms_norm(o.astype(dtype), dn_norm_g) * jax.nn.silu(dn_z.reshape(bsz, length, DN_HEADS, DN_HEAD_V))

    xbc, ssm_conv_new = causal_conv(ssm_xbc, ssm_conv_prev, ssm_conv_w, ssm_conv_b)
    xbc = jax.nn.silu(xbc)
    xs, b_in, c_in = jnp.split(xbc, [D_SSM, D_SSM + SSM_GROUPS * SSM_STATE], axis=-1)
    xs = xs.reshape(bsz, length, SSM_HEADS, SSM_HEAD_DIM)
    b_in = b_in.reshape(bsz, length, SSM_GROUPS, SSM_STATE)
    c_in = c_in.reshape(bsz, length, SSM_GROUPS, SSM_STATE)
    dt = jax.nn.softplus(ssm_dt.astype(jnp.float32) + ssm_dt_bias.astype(jnp.float32))
    a_neg = -jnp.exp(ssm_A_log.astype(jnp.float32))
    y, ssm_h = run_segments(functools.partial(ssd_chunked, a_neg=a_neg), (xs, dt, b_in, c_in), ssm_h0, n_lead)
    y = y.astype(dtype) + ssm_D[:, None] * xs
    y = rms_norm(y.reshape(bsz, length, D_SSM) * jax.nn.silu(ssm_z), ssm_norm_g)

    mixed = jnp.einsum('ble,ed->bld', jnp.concatenate([o.reshape(bsz, length, DN_V_DIM), y], axis=-1), w_out)
    return mixed, (dn_conv_new, dn_s.astype(dtype), ssm_conv_new, ssm_h.astype(dtype))


def peer(x, w_q, keys, u_tab, v_tab):
    n_tok = x.shape[0]
    n_blk = -(-n_tok // PEER_BLOCK)
    xp = jnp.pad(x, ((0, n_blk * PEER_BLOCK - n_tok), (0, 0))).reshape(n_blk, PEER_BLOCK, D_MODEL)

    def block(xb):
        q = jnp.einsum('td,de->te', xb, w_q).reshape(PEER_BLOCK, PEER_HEADS, 2, D_KEY // 2)
        s = jnp.einsum('thcd,hcnd->thcn', q.astype(jnp.float32), keys.astype(jnp.float32))
        sv, si = lax.top_k(s, TOP_K)
        cand = sv[:, :, 0, :, None] + sv[:, :, 1, None, :]
        cv, ci = lax.top_k(cand.reshape(PEER_BLOCK, PEER_HEADS, TOP_K * TOP_K), TOP_K)
        i1 = jnp.take_along_axis(si[:, :, 0], ci // TOP_K, axis=-1)
        i2 = jnp.take_along_axis(si[:, :, 1], ci % TOP_K, axis=-1)
        idx = i1 * N_KEYS + i2
        gate = jax.nn.softmax(cv, axis=-1)
        act = jax.nn.gelu(jnp.einsum('thkd,td->thk', u_tab[idx], xb).astype(jnp.float32), approximate=False)
        return jnp.einsum('thk,thkd->td', (gate * act).astype(xb.dtype), v_tab[idx])

    return lax.map(block, xp).reshape(n_blk * PEER_BLOCK, D_MODEL)[:n_tok]


def setup_inputs(seed: int = 0) -> dict:
    key = jax.random.key(seed)
    ks = iter(jax.random.split(key, 40))

    def nrm(shape, scale):
        return scale * jax.random.normal(next(ks), shape, jnp.float32)

    def dt_bias(shape):
        dt = jnp.exp(jax.random.uniform(next(ks), shape, jnp.float32, math.log(1e-3), math.log(1e-1)))
        return dt + jnp.log(-jnp.expm1(-dt))

    def a_log(shape):
        return jnp.log(jax.random.uniform(next(ks), shape, jnp.float32, 1.0, 16.0))

    def gain(shape):
        return 1.0 + nrm(shape, 0.02)

    L_ = DEPTH
    return {
        'x_prompt': nrm((BATCH, SEQ, D_MODEL), 1.0),
        'x_sample': nrm((DEC_BATCH, DEC_SEQ, D_MODEL), 1.0),
        'state_dn_conv': nrm((L_, DEC_BATCH, CONV_K - 1, DN_QKV_DIM), 1.0),
        'state_dn_rec': nrm((L_, DEC_BATCH, DN_HEADS, DN_HEAD_K, DN_HEAD_V), 0.5),
        'state_ssm_conv': nrm((L_, DEC_BATCH, CONV_K - 1, SSM_XBC_DIM), 1.0),
        'state_ssm_rec': nrm((L_, DEC_BATCH, SSM_HEADS, SSM_HEAD_DIM, SSM_STATE), 0.1),
        'meta_tokens': nrm((N_META, D_MODEL), 1.0),
        'ln_in_g': gain((D_MODEL,)),
        'ln_in_b': nrm((D_MODEL,), 0.02),
        'w_in': nrm((L_, D_MODEL, PROJ_DIM), D_MODEL ** -0.5),
        'dn_conv_w': nrm((L_, CONV_K, DN_QKV_DIM), CONV_K ** -0.5),
        'dn_A_log': a_log((L_, DN_HEADS)),
        'dn_dt_bias': dt_bias((L_, DN_HEADS)),
        'dn_norm_g': gain((L_, DN_HEAD_V)),
        'ssm_conv_w': nrm((L_, CONV_K, SSM_XBC_DIM), CONV_K ** -0.5),
        'ssm_conv_b': nrm((L_, SSM_XBC_DIM), 0.02),
        'ssm_A_log': a_log((L_, SSM_HEADS)),
        'ssm_dt_bias': dt_bias((L_, SSM_HEADS)),
        'ssm_D': gain((L_, SSM_HEADS)),
        'ssm_norm_g': gain((L_, D_SSM)),
        'w_out': nrm((L_, D_MIX, D_MODEL), BETA_INIT * D_MIX ** -0.5),
        'ln1_g': gain((L_, D_MODEL)),
        'ln1_b': nrm((L_, D_MODEL), 0.02),
        'peer_w_q': nrm((L_, D_MODEL, PEER_HEADS * D_KEY), D_MODEL ** -0.5),
        'peer_keys': nrm((L_, PEER_HEADS, 2, N_KEYS, D_KEY // 2), (D_KEY // 2) ** -0.5),
        'peer_u': nrm((L_, N_EXPERTS, D_MODEL), D_MODEL ** -0.5),
        'peer_v': nrm((L_, N_EXPERTS, D_MODEL), BETA_INIT),
        'ln2_g': gain((L_, D_MODEL)),
        'ln2_b': nrm((L_, D_MODEL), 0.02),
    }


def reference(x_prompt, x_sample, state_dn_conv, state_dn_rec, state_ssm_conv, state_ssm_rec,
              meta_tokens, ln_in_g, ln_in_b, w_in, dn_conv_w, dn_A_log, dn_dt_bias, dn_norm_g,
              ssm_conv_w, ssm_conv_b, ssm_A_log, ssm_dt_bias, ssm_D, ssm_norm_g, w_out,
              ln1_g, ln1_b, peer_w_q, peer_keys, peer_u, peer_v, ln2_g, ln2_b):
    dtype = x_prompt.dtype
    bp = x_prompt.shape[0]
    meta = jnp.broadcast_to(meta_tokens.astype(dtype)[None], (bp, N_META, D_MODEL))
    hp = layer_norm(jnp.concatenate([meta, x_prompt], axis=1), ln_in_g, ln_in_b)
    hs = layer_norm(x_sample, ln_in_g, ln_in_b)
    new_p, new_s = [], []
    for l in range(DEPTH):
        lw = (w_in[l], dn_conv_w[l], dn_A_log[l], dn_dt_bias[l], dn_norm_g[l],
              ssm_conv_w[l], ssm_conv_b[l], ssm_A_log[l], ssm_dt_bias[l], ssm_D[l], ssm_norm_g[l], w_out[l])
        zero_p = (jnp.zeros((bp, CONV_K - 1, DN_QKV_DIM), dtype),
                  jnp.zeros((bp, DN_HEADS, DN_HEAD_K, DN_HEAD_V), dtype),
                  jnp.zeros((bp, CONV_K - 1, SSM_XBC_DIM), dtype),
                  jnp.zeros((bp, SSM_HEADS, SSM_HEAD_DIM, SSM_STATE), dtype))
        mix_p, st_p = mixer_sublayer(hp, *zero_p, N_META, *lw)
        mix_s, st_s = mixer_sublayer(hs, state_dn_conv[l], state_dn_rec[l], state_ssm_conv[l], state_ssm_rec[l], 0, *lw)
        new_p.append(st_p)
        new_s.append(st_s)
        hp = layer_norm(ALPHA * hp + mix_p, ln1_g[l], ln1_b[l])
        hs = layer_norm(ALPHA * hs + mix_s, ln1_g[l], ln1_b[l])
        if l == DEPTH - 1:
            hp = hp[:, N_META:]
        n_p = hp.shape[0] * hp.shape[1]
        ff = peer(jnp.concatenate([hp.reshape(-1, D_MODEL), hs.reshape(-1, D_MODEL)], axis=0),
                  peer_w_q[l], peer_keys[l], peer_u[l], peer_v[l])
        hp = layer_norm(ALPHA * hp + ff[:n_p].reshape(hp.shape), ln2_g[l], ln2_b[l])
        hs = layer_norm(ALPHA * hs + ff[n_p:].reshape(hs.shape), ln2_g[l], ln2_b[l])
    new_dn_conv_p = jnp.stack([s[0] for s in new_p], axis=0)
    new_dn_rec_p = jnp.stack([s[1] for s in new_p], axis=0)
    new_ssm_conv_p = jnp.stack([s[2] for s in new_p], axis=0)
    new_ssm_rec_p = jnp.stack([s[3] for s in new_p], axis=0)
    new_dn_conv_s = jnp.stack([s[0] for s in new_s], axis=0)
    new_dn_rec_s = jnp.stack([s[1] for s in new_s], axis=0)
    new_ssm_conv_s = jnp.stack([s[2] for s in new_s], axis=0)
    new_ssm_rec_s = jnp.stack([s[3] for s in new_s], axis=0)
    return (hp, hs, new_dn_conv_p, new_dn_rec_p, new_ssm_conv_p, new_ssm_rec_p,
            new_dn_conv_s, new_dn_rec_s, new_ssm_conv_s, new_ssm_rec_s)
```

```python
import functools
import math

import numpy as np
import jax
import jax.numpy as jnp
from jax import lax
from jax.experimental import pallas as pl
from jax.experimental.pallas import tpu as pltpu

F32 = jnp.float32
BF16 = jnp.bfloat16

D_MODEL = 1024
N_META = 16
DN_HEADS = 16
DN_HEAD = 64
DN_QK = DN_HEADS * DN_HEAD
DN_QKV = 3 * DN_QK
SSM_HEADS = 16
SSM_HEAD = 64
SSM_GROUPS = 2
SSM_STATE = 128
D_SSM = SSM_HEADS * SSM_HEAD
SSM_BC = SSM_GROUPS * SSM_STATE
SSM_XBC = D_SSM + 2 * SSM_BC
HEADS_PER_GROUP = SSM_HEADS // SSM_GROUPS
D_MIX = DN_QK + D_SSM
CONV_K = 4
CHUNK = 64
PEER_HEADS = 8
N_KEYS = 128
D_HALF = 128
TOP_K = 16
ALPHA = 2.0 ** 0.25
GATE_LANES = 128
LANES = 128
VMEM_LIMIT = 56 * 1024 * 1024


def _layer_norm(x, g, b, eps=1e-5):
    mu = jnp.mean(x, axis=-1, keepdims=True)
    xc = x - mu
    var = jnp.mean(xc * xc, axis=-1, keepdims=True)
    return xc * lax.rsqrt(var + eps) * g + b


def _mm(a, b):
    return jnp.dot(a.astype(BF16), b.astype(BF16), preferred_element_type=F32)


def _mm_nt(a, b):
    return lax.dot_general(a.astype(BF16), b.astype(BF16), (((1,), (1,)), ((), ())),
                           preferred_element_type=F32)


def _mm_tn(a, b):
    return lax.dot_general(a.astype(BF16), b.astype(BF16), (((0,), (0,)), ((), ())),
                           preferred_element_type=F32)


def _mm_01(m01, x):
    hi = x.astype(BF16)
    r = x - hi.astype(F32)
    mid = r.astype(BF16)
    lo = (r - mid.astype(F32)).astype(BF16)
    dot = functools.partial(jnp.dot, preferred_element_type=F32)
    return dot(m01, hi) + dot(m01, mid) + dot(m01, lo)


def _softplus(x):
    return jnp.maximum(x, 0.0) + jnp.log1p(jnp.exp(-jnp.abs(x)))


PROJ_SPLITS = (DN_QKV, DN_QK, D_SSM, SSM_XBC, GATE_LANES)
PROJ_COLS = sum(PROJ_SPLITS)


def _ln_proj_kernel(x_ref, g_ref, b_ref, w_ref, qkv_ref, z_ref, sz_ref, xbc_ref, gate_ref):
    xn = _layer_norm(x_ref[...], g_ref[...], b_ref[...]).astype(BF16)
    start = 0
    for out_ref, width in zip((qkv_ref, z_ref, sz_ref, xbc_ref, gate_ref), PROJ_SPLITS):
        out_ref[...] = jnp.dot(xn, w_ref[:, start:start + width], preferred_element_type=F32)
        start += width


def _ln_proj(x, g, b, w, tm):
    t = x.shape[0]
    row = lambda i: (i, 0)
    fixed = lambda i: (0, 0)
    return pl.pallas_call(
        _ln_proj_kernel,
        grid=(t // tm,),
        in_specs=[pl.BlockSpec((tm, D_MODEL), row),
                  pl.BlockSpec((1, D_MODEL), fixed),
                  pl.BlockSpec((1, D_MODEL), fixed),
                  pl.BlockSpec((D_MODEL, PROJ_COLS), fixed, pipeline_mode=pl.Buffered(1))],
        out_specs=[pl.BlockSpec((tm, width), row) for width in PROJ_SPLITS],
        out_shape=[jax.ShapeDtypeStruct((t, width), F32) for width in PROJ_SPLITS],
        compiler_params=pltpu.CompilerParams(dimension_semantics=("parallel",),
                                             vmem_limit_bytes=VMEM_LIMIT),
        name="ln_proj",
    )(x, g, b, w)


CONV_PAD = 8


def _causal_conv(buf_ref, x, w, chunk):
    buf_ref[CONV_PAD:CONV_PAD + chunk, :] = x
    first = CONV_PAD - (CONV_K - 1)
    y = w[0:1] * buf_ref[first:first + chunk, :]
    for i in range(1, CONV_K):
        y = y + w[i:i + 1] * buf_ref[first + i:first + i + chunk, :]
    tail = buf_ref[first + chunk:CONV_PAD + chunk, :]
    buf_ref[first:CONV_PAD, :] = tail
    return y, tail


def _transpose_rows(x, chunk):
    if chunk < LANES:
        x = jnp.concatenate([x, jnp.zeros((LANES - chunk, LANES), F32)], axis=0)
    return x.T[:, :chunk]


def _mixer_kernel(qkv_ref, z_ref, sz_ref, xbc_ref, gate_ref,
                  dnc0_ref, dnr0_ref, ssc0_ref, ssr0_ref,
                  dncw_ref, dna_ref, dndt_ref, dng_ref,
                  sscw_ref, sscb_ref, ssa_ref, ssdt_ref, ssd_ref, ssg_ref,
                  om_ref, dnc_ref, dnr_ref, ssc_ref, ssr_ref,
                  dn_buf, ss_buf, ht_ref, *, chunk, n_chunks):
    c = pl.program_id(1)

    @pl.when(c == 0)
    def _init():
        dn_buf[CONV_PAD - (CONV_K - 1):CONV_PAD, :] = dnc0_ref[0]
        ss_buf[CONV_PAD - (CONV_K - 1):CONV_PAD, :] = ssc0_ref[0]
        dnr_ref[0] = dnr0_ref[0]
        for g in range(SSM_GROUPS):
            rows = jnp.concatenate(
                [ssr0_ref[0, g * HEADS_PER_GROUP + r] for r in range(HEADS_PER_GROUP)], axis=0)
            ht_ref[g] = rows.T

    row = lax.broadcasted_iota(jnp.int32, (chunk, chunk), 0)
    col = lax.broadcasted_iota(jnp.int32, (chunk, chunk), 1)
    causal = row >= col
    strict = row > col
    tril01 = causal.astype(BF16)
    n_squarings = int(math.log2(chunk)) - 1

    gate = gate_ref[...]
    beta = jax.nn.sigmoid(gate[:, 0:DN_HEADS])
    g_dn = -jnp.exp(dna_ref[...]) * _softplus(gate[:, DN_HEADS:2 * DN_HEADS] + dndt_ref[...])
    dt = _softplus(gate[:, 2 * DN_HEADS:2 * DN_HEADS + SSM_HEADS] + ssdt_ref[...])
    da = dt * (-jnp.exp(ssa_ref[...]))
    logdec = jnp.concatenate(
        [g_dn, da, jnp.zeros((chunk, LANES - DN_HEADS - SSM_HEADS), F32)], axis=1)
    cum = _mm_01(tril01, logdec)
    cum_t = _transpose_rows(cum, chunk)
    cum_last = cum[chunk - 1:chunk, :]

    conv, dn_tail = _causal_conv(dn_buf, qkv_ref[...], dncw_ref[...], chunk)
    qkv = jax.nn.silu(conv)
    z = z_ref[...]
    dn_gain = dng_ref[...]
    o_parts = []
    for h in range(DN_HEADS):
        lo = h * DN_HEAD
        qh = qkv[:, lo:lo + DN_HEAD]
        kh = qkv[:, DN_QK + lo:DN_QK + lo + DN_HEAD]
        vh = qkv[:, 2 * DN_QK + lo:2 * DN_QK + lo + DN_HEAD]
        qn = qh * lax.rsqrt(jnp.sum(qh * qh, axis=-1, keepdims=True) + 1e-6) * (DN_HEAD ** -0.5)
        kn = kh * lax.rsqrt(jnp.sum(kh * kh, axis=-1, keepdims=True) + 1e-6)
        cc = cum[:, h:h + 1]
        decay = jnp.exp(jnp.where(causal, cc - cum_t[h:h + 1, :], -jnp.inf))
        bh = beta[:, h:h + 1]
        ecc = jnp.exp(cc)
        a_mat = jnp.where(strict, bh * _mm_nt(kn, kn) * decay, 0.0)
        powers = [a_mat]
        for _ in range(n_squarings):
            powers.append(_mm(powers[-1], powers[-1]))
        sol = jnp.concatenate([bh * vh, (bh * ecc) * kn], axis=1)
        for p in reversed(powers[1:]):
            sol = sol + _mm(p, sol)
        sol = sol - _mm(a_mat, sol)
        u = sol[:, :DN_HEAD]
        w = sol[:, DN_HEAD:]
        s = dnr_ref[0, h]
        delta = u - _mm(w, s)
        qk = jnp.where(causal, _mm_nt(qn, kn) * decay, 0.0)
        o = ecc * _mm(qn, s) + _mm(qk, delta)
        e_last = jnp.exp(cum_last[:, h:h + 1])
        dnr_ref[0, h] = e_last * s + _mm_tn(jnp.exp(cum_last[:, h:h + 1] - cc) * kn, delta)
        on = o * lax.rsqrt(jnp.mean(o * o, axis=-1, keepdims=True) + 1e-6) * dn_gain
        o_parts.append(on * jax.nn.silu(z[:, lo:lo + DN_HEAD]))
    om_ref[:, 0:DN_QK] = jnp.concatenate(o_parts, axis=1)

    conv, ss_tail = _causal_conv(ss_buf, xbc_ref[...], sscw_ref[...], chunk)
    xbc = jax.nn.silu(conv + sscb_ref[...])
    d_skip = ssd_ref[...]
    y_parts = []
    for g in range(SSM_GROUPS):
        b_g = xbc[:, D_SSM + g * SSM_STATE:D_SSM + (g + 1) * SSM_STATE]
        c_g = xbc[:, D_SSM + SSM_BC + g * SSM_STATE:D_SSM + SSM_BC + (g + 1) * SSM_STATE]
        cb = _mm_nt(c_g, b_g)
        ht = ht_ref[g]
        y_inter = _mm(c_g, ht)
        xw_parts = []
        e_last_parts = []
        for r in range(HEADS_PER_GROUP):
            h = g * HEADS_PER_GROUP + r
            lane = DN_HEADS + h
            lo = h * SSM_HEAD
            cc = cum[:, lane:lane + 1]
            lmat = jnp.exp(jnp.where(causal, cc - cum_t[lane:lane + 1, :], -jnp.inf))
            xh = xbc[:, lo:lo + SSM_HEAD]
            xdt = xh * dt[:, h:h + 1]
            y = _mm(lmat * cb, xdt) + jnp.exp(cc) * y_inter[:, r * SSM_HEAD:(r + 1) * SSM_HEAD]
            y_parts.append(y + d_skip[:, h:h + 1] * xh)
            xw_parts.append(xdt * jnp.exp(cum_last[:, lane:lane + 1] - cc))
            e_last_parts.append(
                jnp.broadcast_to(jnp.exp(cum_last[:, lane:lane + 1]), (1, SSM_HEAD)))
        xw = jnp.concatenate(xw_parts, axis=1)
        e_last = jnp.concatenate(e_last_parts, axis=1)
        ht_ref[g] = e_last * ht + _mm_tn(b_g, xw)
    y = jnp.concatenate(y_parts, axis=1) * jax.nn.silu(sz_ref[...])
    y = y * lax.rsqrt(jnp.mean(y * y, axis=-1, keepdims=True) + 1e-6) * ssg_ref[...]
    om_ref[:, DN_QK:D_MIX] = y

    @pl.when(c == n_chunks - 1)
    def _finish():
        dnc_ref[0] = dn_tail
        ssc_ref[0] = ss_tail
        for g in range(SSM_GROUPS):
            rows = ht_ref[g].T
            for r in range(HEADS_PER_GROUP):
                ssr_ref[0, g * HEADS_PER_GROUP + r] = rows[r * SSM_HEAD:(r + 1) * SSM_HEAD, :]


def _mixer(proj, init, weights, n_seq, length, chunk, shared_init):
    n_chunks = length // chunk
    tok = lambda b, c: (b * n_chunks + c, 0)
    if shared_init:
        st3 = lambda b, c: (0, 0, 0)
        st4 = lambda b, c: (0, 0, 0, 0)
    else:
        st3 = lambda b, c: (b, 0, 0)
        st4 = lambda b, c: (b, 0, 0, 0)
    out3 = lambda b, c: (b, 0, 0)
    out4 = lambda b, c: (b, 0, 0, 0)
    fixed = lambda b, c: (0, 0)
    state_shapes = ((CONV_K - 1, DN_QKV), (DN_HEADS, DN_HEAD, DN_HEAD),
                    (CONV_K - 1, SSM_XBC), (SSM_HEADS, SSM_HEAD, SSM_STATE))

    def state_spec(shape, m3, m4):
        return pl.BlockSpec((1,) + shape, m3 if len(shape) == 2 else m4)

    in_specs = [pl.BlockSpec((chunk, width), tok) for width in PROJ_SPLITS]
    in_specs += [state_spec(s, st3, st4) for s in state_shapes]
    in_specs += [pl.BlockSpec(w.shape, fixed) for w in weights]
    out_specs = [pl.BlockSpec((chunk, D_MIX), tok)]
    out_specs += [state_spec(s, out3, out4) for s in state_shapes]
    out_shape = [jax.ShapeDtypeStruct((n_seq * length, D_MIX), F32)]
    out_shape += [jax.ShapeDtypeStruct((n_seq,) + s, F32) for s in state_shapes]
    outs = pl.pallas_call(
        functools.partial(_mixer_kernel, chunk=chunk, n_chunks=n_chunks),
        grid=(n_seq, n_chunks),
        in_specs=in_specs,
        out_specs=out_specs,
        out_shape=out_shape,
        scratch_shapes=[pltpu.VMEM((CONV_PAD + chunk, DN_QKV), F32),
                        pltpu.VMEM((CONV_PAD + chunk, SSM_XBC), F32),
                        pltpu.VMEM((SSM_GROUPS, SSM_STATE, HEADS_PER_GROUP * SSM_HEAD), F32)],
        compiler_params=pltpu.CompilerParams(dimension_semantics=("parallel", "arbitrary"),
                                             vmem_limit_bytes=VMEM_LIMIT),
        name=f"mixer_c{chunk}",
    )(*proj, *init, *weights)
    return outs[0], tuple(outs[1:])


def _out_ln1_kernel(x_ref, om_ref, w_ref, g0_ref, b0_ref, g1_ref, b1_ref, h_ref):
    h0 = _layer_norm(x_ref[...], g0_ref[...], b0_ref[...])
    mix = jnp.dot(om_ref[...].astype(BF16), w_ref[...], preferred_element_type=F32)
    h_ref[...] = _layer_norm(ALPHA * h0 + mix, g1_ref[...], b1_ref[...])


def _out_ln1(x, om, w, g0, b0, g1, b1, tm):
    t = x.shape[0]
    row = lambda i: (i, 0)
    fixed = lambda i: (0, 0)
    vec = pl.BlockSpec((1, D_MODEL), fixed)
    return pl.pallas_call(
        _out_ln1_kernel,
        grid=(t // tm,),
        in_specs=[pl.BlockSpec((tm, D_MODEL), row), pl.BlockSpec((tm, D_MIX), row),
                  pl.BlockSpec((D_MIX, D_MODEL), fixed), vec, vec, vec, vec],
        out_specs=pl.BlockSpec((tm, D_MODEL), row),
        out_shape=jax.ShapeDtypeStruct((t, D_MODEL), F32),
        compiler_params=pltpu.CompilerParams(dimension_semantics=("parallel",),
                                             vmem_limit_bytes=VMEM_LIMIT),
        name="out_ln1",
    )(x, om, w, g0, b0, g1, b1)


PEER_TOKENS = 512
PEER_SUBKEYS = 4
PAIR_LIMITS = tuple(TOP_K // (j + 1) for j in range(TOP_K))


def _top_distinct(s):
    vals, counts = [], []
    for _ in range(TOP_K):
        m = jnp.max(s, axis=0, keepdims=True)
        eq = s == m
        counts.append(jnp.sum(jnp.where(eq, 1.0, 0.0), axis=0, keepdims=True))
        vals.append(m)
        s = jnp.where(eq, -jnp.inf, s)
    return vals, counts


def _peer_route(q1, q2, k1, k2):
    s1 = jnp.dot(k1, q1, preferred_element_type=F32)
    s2 = jnp.dot(k2, q2, preferred_element_type=F32)
    a, na = _top_distinct(s1)
    b, nb = _top_distinct(s2)
    cand, mult, gate = [], [], []
    for j in range(TOP_K):
        ea = jnp.exp(a[j] - a[0])
        for l in range(PAIR_LIMITS[j]):
            cand.append(a[j] + b[l])
            mult.append(na[j] * nb[l])
            gate.append(ea * jnp.exp(b[l] - b[0]))
    cand = jnp.concatenate(cand, axis=0)
    mult = jnp.concatenate(mult, axis=0)
    gate = jnp.concatenate(gate, axis=0)
    theta = jnp.full_like(cand[0:1, :], jnp.inf)
    for i in range(cand.shape[0]):
        ci = cand[i:i + 1, :]
        n_above = jnp.sum(jnp.where(cand > ci, mult, 0.0), axis=0, keepdims=True)
        theta = jnp.where(n_above < TOP_K, jnp.minimum(theta, ci), theta)
    zsum = jnp.sum(jnp.where(cand >= theta, mult * gate, 0.0), axis=0, keepdims=True)
    w1 = jnp.exp(s1 - a[0]) / zsum
    e2 = jnp.exp(s2 - b[0])
    return s1, s2, theta, w1, e2


def _peer_kernel(h_ref, wq_ref, keys_ref, u_ref, vt_ref, g2_ref, b2_ref, y_ref,
                 xt_ref, s1_ref, s2_ref, w1_ref, e2_ref, th_ref, acc_ref, *, n_steps):
    j = pl.program_id(1)

    @pl.when(j == 0)
    def _route():
        xt = h_ref[...].T.astype(BF16)
        xt_ref[...] = xt
        qt = jnp.dot(wq_ref[...], xt, preferred_element_type=F32).astype(BF16)
        for h in range(PEER_HEADS):
            base = h * 2 * D_HALF
            s1, s2, theta, w1, e2 = _peer_route(
                qt[base:base + D_HALF], qt[base + D_HALF:base + 2 * D_HALF],
                keys_ref[2 * h], keys_ref[2 * h + 1])
            s1_ref[h] = s1
            s2_ref[h] = s2
            w1_ref[h] = w1
            e2_ref[h] = e2
            th_ref[h:h + 1, :] = theta
        acc_ref[...] = jnp.zeros_like(acc_ref)

    hid = jnp.dot(u_ref[...], xt_ref[...], preferred_element_type=F32)
    sqrt_half = np.float32(np.sqrt(0.5))
    p_parts = []
    for k in range(PEER_SUBKEYS):
        i1 = j * PEER_SUBKEYS + k
        gate = jnp.zeros((N_KEYS, PEER_TOKENS), F32)
        for h in range(PEER_HEADS):
            pair = s1_ref[h, pl.ds(i1, 1), :] + s2_ref[h]
            sel = jnp.where(pair >= th_ref[h:h + 1, :], e2_ref[h], 0.0)
            gate = gate + sel * w1_ref[h, pl.ds(i1, 1), :]
        hk = hid[k * N_KEYS:(k + 1) * N_KEYS, :]
        act = 0.5 * hk * (1.0 + lax.erf(hk * sqrt_half))
        p_parts.append((gate * act).astype(BF16))
    acc_ref[...] += jnp.dot(vt_ref[...], jnp.concatenate(p_parts, axis=0),
                            preferred_element_type=F32)

    @pl.when(j == n_steps - 1)
    def _finish():
        hres = ALPHA * h_ref[...] + acc_ref[...].T
        y_ref[...] = _layer_norm(hres, g2_ref[...], b2_ref[...])


def _peer(h, wq_t, keys, u, v_t, g2, b2):
    t = h.shape[0]
    n_steps = N_KEYS // PEER_SUBKEYS
    step_experts = PEER_SUBKEYS * N_KEYS
    tok = lambda i, j: (i, 0)
    fixed2 = lambda i, j: (0, 0)
    head_tile = (PEER_HEADS, N_KEYS, PEER_TOKENS)
    return pl.pallas_call(
        functools.partial(_peer_kernel, n_steps=n_steps),
        grid=(t // PEER_TOKENS, n_steps),
        in_specs=[pl.BlockSpec((PEER_TOKENS, D_MODEL), tok),
                  pl.BlockSpec(wq_t.shape, fixed2),
                  pl.BlockSpec(keys.shape, lambda i, j: (0, 0, 0)),
                  pl.BlockSpec((step_experts, D_MODEL), lambda i, j: (j, 0)),
                  pl.BlockSpec((D_MODEL, step_experts), lambda i, j: (0, j)),
                  pl.BlockSpec((1, D_MODEL), fixed2),
                  pl.BlockSpec((1, D_MODEL), fixed2)],
        out_specs=pl.BlockSpec((PEER_TOKENS, D_MODEL), tok),
        out_shape=jax.ShapeDtypeStruct((t, D_MODEL), F32),
        scratch_shapes=[pltpu.VMEM((D_MODEL, PEER_TOKENS), BF16),
                        pltpu.VMEM(head_tile, F32), pltpu.VMEM(head_tile, F32),
                        pltpu.VMEM(head_tile, F32), pltpu.VMEM(head_tile, F32),
                        pltpu.VMEM((PEER_HEADS, PEER_TOKENS), F32),
                        pltpu.VMEM((D_MODEL, PEER_TOKENS), F32)],
        compiler_params=pltpu.CompilerParams(dimension_semantics=("parallel", "arbitrary"),
                                             vmem_limit_bytes=VMEM_LIMIT),
        name="peer",
    )(h, wq_t, keys, u, v_t, g2, b2)


def kernel(x_prompt, x_sample, state_dn_conv, state_dn_rec, state_ssm_conv, state_ssm_rec, meta_tokens, ln_in_g, ln_in_b, w_in, dn_conv_w, dn_A_log, dn_dt_bias, dn_norm_g, ssm_conv_w, ssm_conv_b, ssm_A_log, ssm_dt_bias, ssm_D, ssm_norm_g, w_out, ln1_g, ln1_b, peer_w_q, peer_keys, peer_u, peer_v, ln2_g, ln2_b):
    n_prompt, seq, _ = x_prompt.shape
    n_sample, dec_seq, _ = x_sample.shape
    row = lambda v: v.reshape(1, -1)

    w = w_in[0]
    o_z = DN_QKV
    o_b = o_z + DN_QK
    o_a = o_b + DN_HEADS
    o_sz = o_a + DN_HEADS
    o_xbc = o_sz + D_SSM
    o_dt = o_xbc + SSM_XBC
    gates = jnp.concatenate(
        [w[:, o_b:o_a], w[:, o_a:o_sz], w[:, o_dt:],
         jnp.zeros((D_MODEL, GATE_LANES - 2 * DN_HEADS - SSM_HEADS), w.dtype)], axis=1)
    w_proj = jnp.concatenate(
        [w[:, :o_z], w[:, o_z:o_b], w[:, o_sz:o_xbc], w[:, o_xbc:o_dt], gates],
        axis=1).astype(BF16)
    w_o = w_out[0].astype(BF16)
    wq_t = peer_w_q[0].T.astype(BF16)
    keys = peer_keys[0].reshape(PEER_HEADS * 2, N_KEYS, D_HALF).astype(BF16)
    u_tab = peer_u[0].astype(BF16)
    v_t = peer_v[0].T.astype(BF16)

    g_in, b_in = row(ln_in_g), row(ln_in_b)
    mixer_w = (dn_conv_w[0], row(dn_A_log[0]), row(dn_dt_bias[0]), row(dn_norm_g[0]),
               ssm_conv_w[0], row(ssm_conv_b[0]), row(ssm_A_log[0]), row(ssm_dt_bias[0]),
               row(ssm_D[0]), row(ssm_norm_g[0]))

    def trunk(x, init, n_seq, length, chunk, shared_init):
        proj = _ln_proj(x, g_in, b_in, w_proj, tm=min(256, x.shape[0]))
        om, states = _mixer(proj, init, mixer_w, n_seq, length, chunk, shared_init)
        return om, states

    zero_init = (jnp.zeros((1, CONV_K - 1, DN_QKV), F32),
                 jnp.zeros((1, DN_HEADS, DN_HEAD, DN_HEAD), F32),
                 jnp.zeros((1, CONV_K - 1, SSM_XBC), F32),
                 jnp.zeros((1, SSM_HEADS, SSM_HEAD, SSM_STATE), F32))
    _, meta_state = trunk(meta_tokens, zero_init, 1, N_META, N_META, False)

    xp = x_prompt.reshape(n_prompt * seq, D_MODEL)
    om_p, st_p = trunk(xp, meta_state, n_prompt, seq, CHUNK, True)
    xs = x_sample.reshape(n_sample * dec_seq, D_MODEL)
    init_s = (state_dn_conv[0], state_dn_rec[0], state_ssm_conv[0], state_ssm_rec[0])
    om_s, st_s = trunk(xs, init_s, n_sample, dec_seq, dec_seq, False)

    def tail(x, om):
        h1 = _out_ln1(x, om, w_o, g_in, b_in, row(ln1_g[0]), row(ln1_b[0]), tm=256)
        return _peer(h1, wq_t, keys, u_tab, v_t, row(ln2_g[0]), row(ln2_b[0]))

    y_p = tail(xp, om_p).reshape(n_prompt, seq, D_MODEL)
    y_s = tail(xs, om_s).reshape(n_sample, dec_seq, D_MODEL)
    return (y_p, y_s) + tuple(s[None] for s in st_p) + tuple(s[None] for s in st_s)
```

```python
import functools
import math

import numpy as np
import jax
import jax.numpy as jnp
from jax import lax
from jax.experimental import pallas as pl
from jax.experimental.pallas import tpu as pltpu

F32 = jnp.float32
BF16 = jnp.bfloat16

D_MODEL = 1024
N_META = 16
DN_HEADS = 16
DN_HEAD = 64
DN_QK = DN_HEADS * DN_HEAD
DN_QKV = 3 * DN_QK
SSM_HEADS = 16
SSM_HEAD = 64
SSM_GROUPS = 2
SSM_STATE = 128
D_SSM = SSM_HEADS * SSM_HEAD
SSM_BC = SSM_GROUPS * SSM_STATE
SSM_XBC = D_SSM + 2 * SSM_BC
HEADS_PER_GROUP = SSM_HEADS // SSM_GROUPS
D_MIX = DN_QK + D_SSM
CONV_K = 4
CHUNK = 64
PEER_HEADS = 8
N_KEYS = 128
D_HALF = 128
TOP_K = 16
ALPHA = 2.0 ** 0.25
GATE_LANES = 128
LANES = 128
MXU_DIM = 256
VMEM_LIMIT = 56 * 1024 * 1024


def _layer_norm(x, g, b, eps=1e-5):
    mu = jnp.mean(x, axis=-1, keepdims=True)
    xc = x - mu
    var = jnp.mean(xc * xc, axis=-1, keepdims=True)
    return xc * lax.rsqrt(var + eps) * g + b


def _mm(a, b):
    return jnp.dot(a.astype(BF16), b.astype(BF16), preferred_element_type=F32)


def _mm_nt(a, b):
    return lax.dot_general(a.astype(BF16), b.astype(BF16), (((1,), (1,)), ((), ())),
                           preferred_element_type=F32)


def _mm_tn(a, b):
    return lax.dot_general(a.astype(BF16), b.astype(BF16), (((0,), (0,)), ((), ())),
                           preferred_element_type=F32)


def _split(x):
    hi = x.astype(BF16)
    return hi, (x - hi.astype(F32)).astype(BF16)


def _mm3_split(a, b):
    dot = functools.partial(jnp.dot, preferred_element_type=F32)
    return dot(a[0], b[0]) + (dot(a[0], b[1]) + dot(a[1], b[0]))


def _mm_01(m01, x):
    hi = x.astype(BF16)
    r = x - hi.astype(F32)
    mid = r.astype(BF16)
    lo = (r - mid.astype(F32)).astype(BF16)
    dot = functools.partial(jnp.dot, preferred_element_type=F32)
    return dot(m01, hi) + dot(m01, mid) + dot(m01, lo)


def _softplus(x):
    return jnp.maximum(x, 0.0) + jnp.log1p(jnp.exp(-jnp.abs(x)))


PROJ_SPLITS = (DN_QKV, DN_QK, D_SSM, SSM_XBC, GATE_LANES)
PROJ_COLS = sum(PROJ_SPLITS)


def _ln_proj_kernel(x_ref, g_ref, b_ref, w_ref, qkv_ref, z_ref, sz_ref, xbc_ref, gate_ref):
    xn = _layer_norm(x_ref[...], g_ref[...], b_ref[...]).astype(BF16)
    start = 0
    for out_ref, width in zip((qkv_ref, z_ref, sz_ref, xbc_ref, gate_ref), PROJ_SPLITS):
        out_ref[...] = jnp.dot(xn, w_ref[:, start:start + width], preferred_element_type=F32)
        start += width


def _ln_proj(x, g, b, w, tm):
    t = x.shape[0]
    row = lambda i: (i, 0)
    fixed = lambda i: (0, 0)
    return pl.pallas_call(
        _ln_proj_kernel,
        grid=(t // tm,),
        in_specs=[pl.BlockSpec((tm, D_MODEL), row),
                  pl.BlockSpec((1, D_MODEL), fixed),
                  pl.BlockSpec((1, D_MODEL), fixed),
                  pl.BlockSpec((D_MODEL, PROJ_COLS), fixed, pipeline_mode=pl.Buffered(1))],
        out_specs=[pl.BlockSpec((tm, width), row) for width in PROJ_SPLITS],
        out_shape=[jax.ShapeDtypeStruct((t, width), F32) for width in PROJ_SPLITS],
        compiler_params=pltpu.CompilerParams(dimension_semantics=("parallel",),
                                             vmem_limit_bytes=VMEM_LIMIT),
        name="ln_proj",
    )(x, g, b, w)


CONV_PAD = 8
SOLVE_SPLIT_LEVELS = 3


def _causal_conv(buf_ref, x, w, chunk):
    buf_ref[CONV_PAD:CONV_PAD + chunk, :] = x
    first = CONV_PAD - (CONV_K - 1)
    y = w[0:1] * buf_ref[first:first + chunk, :]
    for i in range(1, CONV_K):
        y = y + w[i:i + 1] * buf_ref[first + i:first + i + chunk, :]
    tail = buf_ref[first + chunk:CONV_PAD + chunk, :]
    buf_ref[first:CONV_PAD, :] = tail
    return y, tail


def _stack_lanes(x, first, count, width):
    return jnp.concatenate(
        [x[:, (first + i) * width:(first + i + 1) * width] for i in range(count)], axis=0)


def _unstack_lanes(xs, count, chunk):
    return jnp.concatenate([xs[i * chunk:(i + 1) * chunk, :] for i in range(count)], axis=1)


def _stack_rows(xt, first, count):
    return jnp.concatenate([xt[first + i:first + i + 1, :] for i in range(count)], axis=1)


def _repeat_rows(v, first, count, reps):
    return jnp.concatenate(
        [jnp.broadcast_to(v[:, first + i:first + i + 1], (reps, 1)) for i in range(count)], axis=0)


def _spread_heads(a, first):
    return jnp.concatenate(
        [jnp.broadcast_to(a[:, first + r:first + r + 1], (a.shape[0], SSM_HEAD))
         for r in range(HEADS_PER_GROUP)], axis=1)


def _block_diag(x, count, own_block):
    return jnp.where(own_block, jnp.concatenate([x] * count, axis=1), 0.0)


def _transpose_rows(x, chunk):
    if chunk < LANES:
        x = jnp.concatenate([x, jnp.zeros((LANES - chunk, LANES), F32)], axis=0)
    return x.T[:, :chunk]


def _mixer_kernel(qkv_ref, z_ref, sz_ref, xbc_ref, gate_ref,
                  dnc0_ref, dnr0_ref, ssc0_ref, ssr0_ref,
                  dncw_ref, dna_ref, dndt_ref, dng_ref,
                  sscw_ref, sscb_ref, ssa_ref, ssdt_ref, ssd_ref, ssg_ref,
                  om_ref, dnc_ref, dnr_ref, ssc_ref, ssr_ref,
                  dn_buf, ss_buf, ht_ref, *, chunk, n_chunks):
    c = pl.program_id(1)

    @pl.when(c == 0)
    def _init():
        dn_buf[CONV_PAD - (CONV_K - 1):CONV_PAD, :] = dnc0_ref[0]
        ss_buf[CONV_PAD - (CONV_K - 1):CONV_PAD, :] = ssc0_ref[0]
        dnr_ref[0] = dnr0_ref[0]
        for g in range(SSM_GROUPS):
            rows = jnp.concatenate(
                [ssr0_ref[0, g * HEADS_PER_GROUP + r] for r in range(HEADS_PER_GROUP)], axis=0)
            ht_ref[g] = rows.T

    stack = min(DN_HEADS, MXU_DIM // chunk)
    ss_stack = min(stack, HEADS_PER_GROUP)
    shift = int(math.log2(chunk))
    n_squarings = shift - 1

    def block_masks(n):
        r_i = lax.broadcasted_iota(jnp.int32, (n, n), 0)
        c_i = lax.broadcasted_iota(jnp.int32, (n, n), 1)
        same = (r_i >> shift) == (c_i >> shift)
        return same & (r_i >= c_i), same & (r_i > c_i)

    causal, strict = block_masks(stack * chunk)
    causal_ss = causal if ss_stack == stack else block_masks(ss_stack * chunk)[0]
    head_of_row = lax.broadcasted_iota(jnp.int32, (stack * chunk, stack * DN_HEAD), 0) >> shift
    head_of_lane = (lax.broadcasted_iota(jnp.int32, (stack * chunk, stack * DN_HEAD), 1)
                    >> int(math.log2(DN_HEAD)))
    own_head = head_of_row == head_of_lane
    tril01 = (lax.broadcasted_iota(jnp.int32, (chunk, chunk), 0)
              >= lax.broadcasted_iota(jnp.int32, (chunk, chunk), 1)).astype(BF16)

    gate = gate_ref[...]
    beta = jax.nn.sigmoid(gate[:, 0:DN_HEADS])
    g_dn = -jnp.exp(dna_ref[...]) * _softplus(gate[:, DN_HEADS:2 * DN_HEADS] + dndt_ref[...])
    dt = _softplus(gate[:, 2 * DN_HEADS:2 * DN_HEADS + SSM_HEADS] + ssdt_ref[...])
    da = dt * (-jnp.exp(ssa_ref[...]))
    logdec = jnp.concatenate(
        [g_dn, da, jnp.zeros((chunk, LANES - DN_HEADS - SSM_HEADS), F32)], axis=1)
    cum = _mm_01(tril01, logdec)
    cum_t = _transpose_rows(cum, chunk)
    cum_last = cum[chunk - 1:chunk, :]

    conv, dn_tail = _causal_conv(dn_buf, qkv_ref[...], dncw_ref[...], chunk)
    qkv = jax.nn.silu(conv)
    dn_groups = []
    for g in range(DN_HEADS // stack):
        h0 = g * stack
        qs = _stack_lanes(qkv, h0, stack, DN_HEAD)
        ks = _stack_lanes(qkv, DN_HEADS + h0, stack, DN_HEAD)
        vs = _stack_lanes(qkv, 2 * DN_HEADS + h0, stack, DN_HEAD)
        qn = qs * lax.rsqrt(jnp.sum(qs * qs, axis=-1, keepdims=True) + 1e-6) * (DN_HEAD ** -0.5)
        kn = ks * lax.rsqrt(jnp.sum(ks * ks, axis=-1, keepdims=True) + 1e-6)
        cc = _stack_lanes(cum, h0, stack, 1)
        bcol = _stack_lanes(beta, h0, stack, 1)
        decay = jnp.exp(jnp.where(causal, cc - _stack_rows(cum_t, h0, stack), -jnp.inf))
        ecc = jnp.exp(cc)
        a_mat = jnp.where(strict, bcol * _mm_nt(kn, kn) * decay, 0.0)
        rhs = jnp.concatenate([bcol * vs, (bcol * ecc) * kn], axis=1)
        qk = jnp.where(causal, _mm_nt(qn, kn) * decay, 0.0)
        kd = jnp.exp(_repeat_rows(cum_last, h0, stack, chunk) - cc) * kn
        dn_groups.append((qn, ecc, a_mat, rhs, qk, kd))

    pows = [_split(grp[2]) for grp in dn_groups]
    sols = [grp[3] - _mm3_split(p, _split(grp[3])) for p, grp in zip(pows, dn_groups)]
    for level in range(n_squarings):
        if level < SOLVE_SPLIT_LEVELS:
            pows = [_split(_mm3_split(p, p)) for p in pows]
            sols = [s + _mm3_split(p, _split(s)) for p, s in zip(pows, sols)]
        else:
            pows = [(jnp.dot(p[0], p[0], preferred_element_type=F32).astype(BF16),) for p in pows]
            sols = [s + jnp.dot(p[0], s.astype(BF16), preferred_element_type=F32)
                    for p, s in zip(pows, sols)]

    conv, ss_tail = _causal_conv(ss_buf, xbc_ref[...], sscw_ref[...], chunk)
    xbc = jax.nn.silu(conv + sscb_ref[...])
    cum_ss = cum[:, DN_HEADS:DN_HEADS + SSM_HEADS]
    end_ss = cum_last[:, DN_HEADS:DN_HEADS + SSM_HEADS]
    ecum_ss = jnp.exp(cum_ss)
    tok_w = dt * jnp.exp(end_ss - cum_ss)
    e_end_ss = jnp.exp(end_ss)
    d_skip = ssd_ref[...]
    y_groups = []
    for g in range(SSM_GROUPS):
        g0 = g * HEADS_PER_GROUP
        b_g = xbc[:, D_SSM + g * SSM_STATE:D_SSM + (g + 1) * SSM_STATE]
        c_g = xbc[:, D_SSM + SSM_BC + g * SSM_STATE:D_SSM + SSM_BC + (g + 1) * SSM_STATE]
        cb = _mm_nt(c_g, b_g)
        cb_wide = jnp.concatenate([cb] * ss_stack, axis=1)
        cb_tiled = jnp.concatenate([cb_wide] * ss_stack, axis=0)
        intra = []
        for h0 in range(g0, g0 + HEADS_PER_GROUP, ss_stack):
            cc = _stack_lanes(cum_ss, h0, ss_stack, 1)
            lmat = jnp.exp(jnp.where(causal_ss, cc - _stack_rows(cum_t, DN_HEADS + h0, ss_stack),
                                     -jnp.inf))
            xdt = _stack_lanes(xbc, h0, ss_stack, SSM_HEAD) * _stack_lanes(dt, h0, ss_stack, 1)
            intra.append(_unstack_lanes(_mm(lmat * cb_tiled, xdt), ss_stack, chunk))
        x_g = xbc[:, g0 * SSM_HEAD:(g0 + HEADS_PER_GROUP) * SSM_HEAD]
        ht = ht_ref[g]
        y_g = (jnp.concatenate(intra, axis=1)
               + _spread_heads(ecum_ss, g0) * _mm(c_g, ht)
               + _spread_heads(d_skip, g0) * x_g)
        y_groups.append(y_g)
        xw = x_g * _spread_heads(tok_w, g0)
        ht_ref[g] = _spread_heads(e_end_ss, g0) * ht + _mm_tn(b_g, xw)
    y = jnp.concatenate(y_groups, axis=1) * jax.nn.silu(sz_ref[...])
    y = y * lax.rsqrt(jnp.mean(y * y, axis=-1, keepdims=True) + 1e-6) * ssg_ref[...]
    om_ref[:, DN_QK:D_MIX] = y

    dn_gain = dng_ref[...]
    e_end_dn = jnp.exp(cum_last)
    o_groups = []
    n_rows = stack * chunk
    for g, (qn, ecc, _, _, qk, kd) in enumerate(dn_groups):
        h0 = g * stack
        s = jnp.concatenate([dnr_ref[0, h0 + i] for i in range(stack)], axis=0)
        u = sols[g][:, :DN_HEAD]
        w = sols[g][:, DN_HEAD:]
        wq = _mm(jnp.concatenate([_block_diag(w, stack, own_head),
                                  _block_diag(qn, stack, own_head)], axis=0), s)
        delta = u - wq[:n_rows]
        o = ecc * wq[n_rows:] + _mm(qk, delta)
        s_new = (_repeat_rows(e_end_dn, h0, stack, DN_HEAD) * s
                 + _mm_tn(_block_diag(kd, stack, own_head), delta))
        for i in range(stack):
            dnr_ref[0, h0 + i] = s_new[i * DN_HEAD:(i + 1) * DN_HEAD, :]
        on = o * lax.rsqrt(jnp.mean(o * o, axis=-1, keepdims=True) + 1e-6) * dn_gain
        o_groups.append(_unstack_lanes(on, stack, chunk))
    om_ref[:, 0:DN_QK] = jnp.concatenate(o_groups, axis=1) * jax.nn.silu(z_ref[...])

    @pl.when(c == n_chunks - 1)
    def _finish():
        dnc_ref[0] = dn_tail
        ssc_ref[0] = ss_tail
        for g in range(SSM_GROUPS):
            rows = ht_ref[g].T
            for r in range(HEADS_PER_GROUP):
                ssr_ref[0, g * HEADS_PER_GROUP + r] = rows[r * SSM_HEAD:(r + 1) * SSM_HEAD, :]


def _mixer(proj, init, weights, n_seq, length, chunk, shared_init):
    n_chunks = length // chunk
    tok = lambda b, c: (b * n_chunks + c, 0)
    if shared_init:
        st3 = lambda b, c: (0, 0, 0)
        st4 = lambda b, c: (0, 0, 0, 0)
    else:
        st3 = lambda b, c: (b, 0, 0)
        st4 = lambda b, c: (b, 0, 0, 0)
    out3 = lambda b, c: (b, 0, 0)
    out4 = lambda b, c: (b, 0, 0, 0)
    fixed = lambda b, c: (0, 0)
    state_shapes = ((CONV_K - 1, DN_QKV), (DN_HEADS, DN_HEAD, DN_HEAD),
                    (CONV_K - 1, SSM_XBC), (SSM_HEADS, SSM_HEAD, SSM_STATE))

    def state_spec(shape, m3, m4):
        return pl.BlockSpec((1,) + shape, m3 if len(shape) == 2 else m4)

    in_specs = [pl.BlockSpec((chunk, width), tok) for width in PROJ_SPLITS]
    in_specs += [state_spec(s, st3, st4) for s in state_shapes]
    in_specs += [pl.BlockSpec(w.shape, fixed) for w in weights]
    out_specs = [pl.BlockSpec((chunk, D_MIX), tok)]
    out_specs += [state_spec(s, out3, out4) for s in state_shapes]
    out_shape = [jax.ShapeDtypeStruct((n_seq * length, D_MIX), F32)]
    out_shape += [jax.ShapeDtypeStruct((n_seq,) + s, F32) for s in state_shapes]
    outs = pl.pallas_call(
        functools.partial(_mixer_kernel, chunk=chunk, n_chunks=n_chunks),
        grid=(n_seq, n_chunks),
        in_specs=in_specs,
        out_specs=out_specs,
        out_shape=out_shape,
        scratch_shapes=[pltpu.VMEM((CONV_PAD + chunk, DN_QKV), F32),
                        pltpu.VMEM((CONV_PAD + chunk, SSM_XBC), F32),
                        pltpu.VMEM((SSM_GROUPS, SSM_STATE, HEADS_PER_GROUP * SSM_HEAD), F32)],
        compiler_params=pltpu.CompilerParams(dimension_semantics=("parallel", "arbitrary"),
                                             vmem_limit_bytes=VMEM_LIMIT),
        name=f"mixer_c{chunk}",
    )(*proj, *init, *weights)
    return outs[0], tuple(outs[1:])


def _out_ln1_kernel(x_ref, om_ref, w_ref, g0_ref, b0_ref, g1_ref, b1_ref, h_ref):
    h0 = _layer_norm(x_ref[...], g0_ref[...], b0_ref[...])
    mix = jnp.dot(om_ref[...].astype(BF16), w_ref[...], preferred_element_type=F32)
    h_ref[...] = _layer_norm(ALPHA * h0 + mix, g1_ref[...], b1_ref[...])


def _out_ln1(x, om, w, g0, b0, g1, b1, tm):
    t = x.shape[0]
    row = lambda i: (i, 0)
    fixed = lambda i: (0, 0)
    vec = pl.BlockSpec((1, D_MODEL), fixed)
    return pl.pallas_call(
        _out_ln1_kernel,
        grid=(t // tm,),
        in_specs=[pl.BlockSpec((tm, D_MODEL), row), pl.BlockSpec((tm, D_MIX), row),
                  pl.BlockSpec((D_MIX, D_MODEL), fixed), vec, vec, vec, vec],
        out_specs=pl.BlockSpec((tm, D_MODEL), row),
        out_shape=jax.ShapeDtypeStruct((t, D_MODEL), F32),
        compiler_params=pltpu.CompilerParams(dimension_semantics=("parallel",),
                                             vmem_limit_bytes=VMEM_LIMIT),
        name="out_ln1",
    )(x, om, w, g0, b0, g1, b1)


PEER_TOKENS = 512
PEER_SUBKEYS = 4
PAIR_LIMITS = tuple(TOP_K // (j + 1) for j in range(TOP_K))


def _top_distinct(s):
    vals, counts = [], []
    for _ in range(TOP_K):
        m = jnp.max(s, axis=0, keepdims=True)
        eq = s == m
        counts.append(jnp.sum(jnp.where(eq, 1.0, 0.0), axis=0, keepdims=True))
        vals.append(m)
        s = jnp.where(eq, -jnp.inf, s)
    return vals, counts


def _peer_route(q1, q2, k1, k2):
    s1 = jnp.dot(k1, q1, preferred_element_type=F32)
    s2 = jnp.dot(k2, q2, preferred_element_type=F32)
    a, na = _top_distinct(s1)
    b, nb = _top_distinct(s2)
    cand, mult, gate = [], [], []
    for j in range(TOP_K):
        ea = jnp.exp(a[j] - a[0])
        for l in range(PAIR_LIMITS[j]):
            cand.append(a[j] + b[l])
            mult.append(na[j] * nb[l])
            gate.append(ea * jnp.exp(b[l] - b[0]))
    cand = jnp.concatenate(cand, axis=0)
    mult = jnp.concatenate(mult, axis=0)
    gate = jnp.concatenate(gate, axis=0)
    theta = jnp.full_like(cand[0:1, :], jnp.inf)
    for i in range(cand.shape[0]):
        ci = cand[i:i + 1, :]
        n_above = jnp.sum(jnp.where(cand > ci, mult, 0.0), axis=0, keepdims=True)
        theta = jnp.where(n_above < TOP_K, jnp.minimum(theta, ci), theta)
    zsum = jnp.sum(jnp.where(cand >= theta, mult * gate, 0.0), axis=0, keepdims=True)
    w1 = jnp.exp(s1 - a[0]) / zsum
    e2 = jnp.exp(s2 - b[0])
    return s1, s2, theta, w1, e2


def _peer_kernel(h_ref, wq_ref, keys_ref, u_ref, vt_ref, g2_ref, b2_ref, y_ref,
                 xt_ref, s1_ref, s2_ref, w1_ref, e2_ref, th_ref, acc_ref, *, n_steps):
    j = pl.program_id(1)

    @pl.when(j == 0)
    def _route():
        xt = h_ref[...].T.astype(BF16)
        xt_ref[...] = xt
        qt = jnp.dot(wq_ref[...], xt, preferred_element_type=F32).astype(BF16)
        for h in range(PEER_HEADS):
            base = h * 2 * D_HALF
            s1, s2, theta, w1, e2 = _peer_route(
                qt[base:base + D_HALF], qt[base + D_HALF:base + 2 * D_HALF],
                keys_ref[2 * h], keys_ref[2 * h + 1])
            s1_ref[h] = s1
            s2_ref[h] = s2
            w1_ref[h] = w1
            e2_ref[h] = e2
            th_ref[h:h + 1, :] = theta
        acc_ref[...] = jnp.zeros_like(acc_ref)

    hid = jnp.dot(u_ref[...], xt_ref[...], preferred_element_type=F32)
    sqrt_half = np.float32(np.sqrt(0.5))
    p_parts = []
    for k in range(PEER_SUBKEYS):
        i1 = j * PEER_SUBKEYS + k
        gate = jnp.zeros((N_KEYS, PEER_TOKENS), F32)
        for h in range(PEER_HEADS):
            pair = s1_ref[h, pl.ds(i1, 1), :] + s2_ref[h]
            sel = jnp.where(pair >= th_ref[h:h + 1, :], e2_ref[h], 0.0)
            gate = gate + sel * w1_ref[h, pl.ds(i1, 1), :]
        hk = hid[k * N_KEYS:(k + 1) * N_KEYS, :]
        act = 0.5 * hk * (1.0 + lax.erf(hk * sqrt_half))
        p_parts.append((gate * act).astype(BF16))
    acc_ref[...] += jnp.dot(vt_ref[...], jnp.concatenate(p_parts, axis=0),
                            preferred_element_type=F32)

    @pl.when(j == n_steps - 1)
    def _finish():
        hres = ALPHA * h_ref[...] + acc_ref[...].T
        y_ref[...] = _layer_norm(hres, g2_ref[...], b2_ref[...])


def _peer(h, wq_t, keys, u, v_t, g2, b2):
    t = h.shape[0]
    n_steps = N_KEYS // PEER_SUBKEYS
    step_experts = PEER_SUBKEYS * N_KEYS
    tok = lambda i, j: (i, 0)
    fixed2 = lambda i, j: (0, 0)
    head_tile = (PEER_HEADS, N_KEYS, PEER_TOKENS)
    return pl.pallas_call(
        functools.partial(_peer_kernel, n_steps=n_steps),
        grid=(t // PEER_TOKENS, n_steps),
        in_specs=[pl.BlockSpec((PEER_TOKENS, D_MODEL), tok),
                  pl.BlockSpec(wq_t.shape, fixed2),
                  pl.BlockSpec(keys.shape, lambda i, j: (0, 0, 0)),
                  pl.BlockSpec((step_experts, D_MODEL), lambda i, j: (j, 0)),
                  pl.BlockSpec((D_MODEL, step_experts), lambda i, j: (0, j)),
                  pl.BlockSpec((1, D_MODEL), fixed2),
                  pl.BlockSpec((1, D_MODEL), fixed2)],
        out_specs=pl.BlockSpec((PEER_TOKENS, D_MODEL), tok),
        out_shape=jax.ShapeDtypeStruct((t, D_MODEL), F32),
        scratch_shapes=[pltpu.VMEM((D_MODEL, PEER_TOKENS), BF16),
                        pltpu.VMEM(head_tile, F32), pltpu.VMEM(head_tile, F32),
                        pltpu.VMEM(head_tile, F32), pltpu.VMEM(head_tile, F32),
                        pltpu.VMEM((PEER_HEADS, PEER_TOKENS), F32),
                        pltpu.VMEM((D_MODEL, PEER_TOKENS), F32)],
        compiler_params=pltpu.CompilerParams(dimension_semantics=("parallel", "arbitrary"),
                                             vmem_limit_bytes=VMEM_LIMIT),
        name="peer",
    )(h, wq_t, keys, u, v_t, g2, b2)


def kernel(x_prompt, x_sample, state_dn_conv, state_dn_rec, state_ssm_conv, state_ssm_rec, meta_tokens, ln_in_g, ln_in_b, w_in, dn_conv_w, dn_A_log, dn_dt_bias, dn_norm_g, ssm_conv_w, ssm_conv_b, ssm_A_log, ssm_dt_bias, ssm_D, ssm_norm_g, w_out, ln1_g, ln1_b, peer_w_q, peer_keys, peer_u, peer_v, ln2_g, ln2_b):
    n_prompt, seq, _ = x_prompt.shape
    n_sample, dec_seq, _ = x_sample.shape
    row = lambda v: v.reshape(1, -1)

    w = w_in[0]
    o_z = DN_QKV
    o_b = o_z + DN_QK
    o_a = o_b + DN_HEADS
    o_sz = o_a + DN_HEADS
    o_xbc = o_sz + D_SSM
    o_dt = o_xbc + SSM_XBC
    gates = jnp.concatenate(
        [w[:, o_b:o_a], w[:, o_a:o_sz], w[:, o_dt:],
         jnp.zeros((D_MODEL, GATE_LANES - 2 * DN_HEADS - SSM_HEADS), w.dtype)], axis=1)
    w_proj = jnp.concatenate(
        [w[:, :o_z], w[:, o_z:o_b], w[:, o_sz:o_xbc], w[:, o_xbc:o_dt], gates],
        axis=1).astype(BF16)
    w_o = w_out[0].astype(BF16)
    wq_t = peer_w_q[0].T.astype(BF16)
    keys = peer_keys[0].reshape(PEER_HEADS * 2, N_KEYS, D_HALF).astype(BF16)
    u_tab = peer_u[0].astype(BF16)
    v_t = peer_v[0].T.astype(BF16)

    g_in, b_in = row(ln_in_g), row(ln_in_b)
    mixer_w = (dn_conv_w[0], row(dn_A_log[0]), row(dn_dt_bias[0]), row(dn_norm_g[0]),
               ssm_conv_w[0], row(ssm_conv_b[0]), row(ssm_A_log[0]), row(ssm_dt_bias[0]),
               row(ssm_D[0]), row(ssm_norm_g[0]))

    def trunk(x, init, n_seq, length, chunk, shared_init):
        proj = _ln_proj(x, g_in, b_in, w_proj, tm=min(256, x.shape[0]))
        om, states = _mixer(proj, init, mixer_w, n_seq, length, chunk, shared_init)
        return om, states

    zero_init = (jnp.zeros((1, CONV_K - 1, DN_QKV), F32),
                 jnp.zeros((1, DN_HEADS, DN_HEAD, DN_HEAD), F32),
                 jnp.zeros((1, CONV_K - 1, SSM_XBC), F32),
                 jnp.zeros((1, SSM_HEADS, SSM_HEAD, SSM_STATE), F32))
    _, meta_state = trunk(meta_tokens, zero_init, 1, N_META, N_META, False)

    xp = x_prompt.reshape(n_prompt * seq, D_MODEL)
    om_p, st_p = trunk(xp, meta_state, n_prompt, seq, CHUNK, True)
    xs = x_sample.reshape(n_sample * dec_seq, D_MODEL)
    init_s = (state_dn_conv[0], state_dn_rec[0], state_ssm_conv[0], state_ssm_rec[0])
    om_s, st_s = trunk(xs, init_s, n_sample, dec_seq, dec_seq, False)

    def tail(x, om):
        h1 = _out_ln1(x, om, w_o, g_in, b_in, row(ln1_g[0]), row(ln1_b[0]), tm=256)
        return _peer(h1, wq_t, keys, u_tab, v_t, row(ln2_g[0]), row(ln2_b[0]))

    y_p = tail(xp, om_p).reshape(n_prompt, seq, D_MODEL)
    y_s = tail(xs, om_s).reshape(n_sample, dec_seq, D_MODEL)
    return (y_p, y_s) + tuple(s[None] for s in st_p) + tuple(s[None] for s in st_s)
```

```python
import functools
import math

import numpy as np
import jax
import jax.numpy as jnp
from jax import lax
from jax.experimental import pallas as pl
from jax.experimental.pallas import tpu as pltpu

F32 = jnp.float32
BF16 = jnp.bfloat16

D_MODEL = 1024
N_META = 16
DN_HEADS = 16
DN_HEAD = 64
DN_QK = DN_HEADS * DN_HEAD
DN_QKV = 3 * DN_QK
SSM_HEADS = 16
SSM_HEAD = 64
SSM_GROUPS = 2
SSM_STATE = 128
D_SSM = SSM_HEADS * SSM_HEAD
SSM_BC = SSM_GROUPS * SSM_STATE
SSM_XBC = D_SSM + 2 * SSM_BC
HEADS_PER_GROUP = SSM_HEADS // SSM_GROUPS
D_MIX = DN_QK + D_SSM
CONV_K = 4
CHUNK = 64
PEER_HEADS = 8
N_KEYS = 128
D_HALF = 128
TOP_K = 16
ALPHA = 2.0 ** 0.25
GATE_LANES = 128
LANES = 128
MXU_DIM = 256
VMEM_LIMIT = 56 * 1024 * 1024


def _layer_norm(x, g, b, eps=1e-5):
    mu = jnp.mean(x, axis=-1, keepdims=True)
    xc = x - mu
    var = jnp.mean(xc * xc, axis=-1, keepdims=True)
    return xc * lax.rsqrt(var + eps) * g + b


def _mm(a, b):
    return jnp.dot(a.astype(BF16), b.astype(BF16), preferred_element_type=F32)


def _mm_nt(a, b):
    return lax.dot_general(a.astype(BF16), b.astype(BF16), (((1,), (1,)), ((), ())),
                           preferred_element_type=F32)


def _mm_tn(a, b):
    return lax.dot_general(a.astype(BF16), b.astype(BF16), (((0,), (0,)), ((), ())),
                           preferred_element_type=F32)


def _split(x):
    hi = x.astype(BF16)
    return hi, (x - hi.astype(F32)).astype(BF16)


def _mm3_split(a, b):
    dot = functools.partial(jnp.dot, preferred_element_type=F32)
    return dot(a[0], b[0]) + (dot(a[0], b[1]) + dot(a[1], b[0]))


def _mm_01(m01, x):
    hi = x.astype(BF16)
    r = x - hi.astype(F32)
    mid = r.astype(BF16)
    lo = (r - mid.astype(F32)).astype(BF16)
    dot = functools.partial(jnp.dot, preferred_element_type=F32)
    return dot(m01, hi) + dot(m01, mid) + dot(m01, lo)


def _softplus(x):
    return jnp.maximum(x, 0.0) + jnp.log1p(jnp.exp(-jnp.abs(x)))


PROJ_SPLITS = (DN_QKV, DN_QK, D_SSM, SSM_XBC, GATE_LANES)
PROJ_COLS = sum(PROJ_SPLITS)


def _ln_proj_kernel(x_ref, g_ref, b_ref, w_ref, qkv_ref, z_ref, sz_ref, xbc_ref, gate_ref):
    xn = _layer_norm(x_ref[...], g_ref[...], b_ref[...]).astype(BF16)
    start = 0
    for out_ref, width in zip((qkv_ref, z_ref, sz_ref, xbc_ref, gate_ref), PROJ_SPLITS):
        out_ref[...] = jnp.dot(xn, w_ref[:, start:start + width], preferred_element_type=F32)
        start += width


def _ln_proj(x, g, b, w, tm):
    t = x.shape[0]
    row = lambda i: (i, 0)
    fixed = lambda i: (0, 0)
    return pl.pallas_call(
        _ln_proj_kernel,
        grid=(t // tm,),
        in_specs=[pl.BlockSpec((tm, D_MODEL), row),
                  pl.BlockSpec((1, D_MODEL), fixed),
                  pl.BlockSpec((1, D_MODEL), fixed),
                  pl.BlockSpec((D_MODEL, PROJ_COLS), fixed, pipeline_mode=pl.Buffered(1))],
        out_specs=[pl.BlockSpec((tm, width), row) for width in PROJ_SPLITS],
        out_shape=[jax.ShapeDtypeStruct((t, width), F32) for width in PROJ_SPLITS],
        compiler_params=pltpu.CompilerParams(dimension_semantics=("parallel",),
                                             vmem_limit_bytes=VMEM_LIMIT),
        name="ln_proj",
    )(x, g, b, w)


CONV_PAD = 8
SOLVE_SPLIT_LEVELS = 3


def _causal_conv(buf_ref, x, w, chunk):
    buf_ref[CONV_PAD:CONV_PAD + chunk, :] = x
    first = CONV_PAD - (CONV_K - 1)
    y = w[0:1] * buf_ref[first:first + chunk, :]
    for i in range(1, CONV_K):
        y = y + w[i:i + 1] * buf_ref[first + i:first + i + chunk, :]
    tail = buf_ref[first + chunk:CONV_PAD + chunk, :]
    buf_ref[first:CONV_PAD, :] = tail
    return y, tail


def _stack_lanes(x, first, count, width):
    return jnp.concatenate(
        [x[:, (first + i) * width:(first + i + 1) * width] for i in range(count)], axis=0)


def _unstack_lanes(xs, count, chunk):
    return jnp.concatenate([xs[i * chunk:(i + 1) * chunk, :] for i in range(count)], axis=1)


def _stack_rows(xt, first, count):
    return jnp.concatenate([xt[first + i:first + i + 1, :] for i in range(count)], axis=1)


def _repeat_rows(v, first, count, reps):
    return jnp.concatenate(
        [jnp.broadcast_to(v[:, first + i:first + i + 1], (reps, 1)) for i in range(count)], axis=0)


def _spread_heads(a, first):
    return jnp.concatenate(
        [jnp.broadcast_to(a[:, first + r:first + r + 1], (a.shape[0], SSM_HEAD))
         for r in range(HEADS_PER_GROUP)], axis=1)


def _block_diag(x, count, own_block):
    return jnp.where(own_block, jnp.concatenate([x] * count, axis=1), 0.0)


def _transpose_rows(x, chunk):
    if chunk < LANES:
        x = jnp.concatenate([x, jnp.zeros((LANES - chunk, LANES), F32)], axis=0)
    return x.T[:, :chunk]


def _mixer_kernel(qkv_ref, z_ref, sz_ref, xbc_ref, gate_ref,
                  dnc0_ref, dnr0_ref, ssc0_ref, ssr0_ref,
                  dncw_ref, dna_ref, dndt_ref, dng_ref,
                  sscw_ref, sscb_ref, ssa_ref, ssdt_ref, ssd_ref, ssg_ref,
                  om_ref, dnc_ref, dnr_ref, ssc_ref, ssr_ref,
                  dn_buf, ss_buf, ht_ref, *, chunk, n_chunks):
    c = pl.program_id(1)

    @pl.when(c == 0)
    def _init():
        dn_buf[CONV_PAD - (CONV_K - 1):CONV_PAD, :] = dnc0_ref[0]
        ss_buf[CONV_PAD - (CONV_K - 1):CONV_PAD, :] = ssc0_ref[0]
        dnr_ref[0] = dnr0_ref[0]
        for g in range(SSM_GROUPS):
            rows = jnp.concatenate(
                [ssr0_ref[0, g * HEADS_PER_GROUP + r] for r in range(HEADS_PER_GROUP)], axis=0)
            ht_ref[g] = rows.T

    stack = min(DN_HEADS, MXU_DIM // chunk)
    ss_stack = min(stack, HEADS_PER_GROUP)
    shift = int(math.log2(chunk))
    n_squarings = shift - 1

    def block_masks(n):
        r_i = lax.broadcasted_iota(jnp.int32, (n, n), 0)
        c_i = lax.broadcasted_iota(jnp.int32, (n, n), 1)
        same = (r_i >> shift) == (c_i >> shift)
        return same & (r_i >= c_i), same & (r_i > c_i)

    causal, strict = block_masks(stack * chunk)
    causal_ss = causal if ss_stack == stack else block_masks(ss_stack * chunk)[0]
    head_of_row = lax.broadcasted_iota(jnp.int32, (stack * chunk, stack * DN_HEAD), 0) >> shift
    head_of_lane = (lax.broadcasted_iota(jnp.int32, (stack * chunk, stack * DN_HEAD), 1)
                    >> int(math.log2(DN_HEAD)))
    own_head = head_of_row == head_of_lane
    tril01 = (lax.broadcasted_iota(jnp.int32, (chunk, chunk), 0)
              >= lax.broadcasted_iota(jnp.int32, (chunk, chunk), 1)).astype(BF16)

    gate = gate_ref[...]
    beta = jax.nn.sigmoid(gate[:, 0:DN_HEADS])
    g_dn = -jnp.exp(dna_ref[...]) * _softplus(gate[:, DN_HEADS:2 * DN_HEADS] + dndt_ref[...])
    dt = _softplus(gate[:, 2 * DN_HEADS:2 * DN_HEADS + SSM_HEADS] + ssdt_ref[...])
    da = dt * (-jnp.exp(ssa_ref[...]))
    logdec = jnp.concatenate(
        [g_dn, da, jnp.zeros((chunk, LANES - DN_HEADS - SSM_HEADS), F32)], axis=1)
    cum = _mm_01(tril01, logdec)
    cum_t = _transpose_rows(cum, chunk)
    cum_last = cum[chunk - 1:chunk, :]

    conv, dn_tail = _causal_conv(dn_buf, qkv_ref[...], dncw_ref[...], chunk)
    qkv = jax.nn.silu(conv)
    dn_groups = []
    for g in range(DN_HEADS // stack):
        h0 = g * stack
        qs = _stack_lanes(qkv, h0, stack, DN_HEAD)
        ks = _stack_lanes(qkv, DN_HEADS + h0, stack, DN_HEAD)
        vs = _stack_lanes(qkv, 2 * DN_HEADS + h0, stack, DN_HEAD)
        qn = qs * lax.rsqrt(jnp.sum(qs * qs, axis=-1, keepdims=True) + 1e-6) * (DN_HEAD ** -0.5)
        kn = ks * lax.rsqrt(jnp.sum(ks * ks, axis=-1, keepdims=True) + 1e-6)
        cc = _stack_lanes(cum, h0, stack, 1)
        bcol = _stack_lanes(beta, h0, stack, 1)
        decay = jnp.exp(jnp.where(causal, cc - _stack_rows(cum_t, h0, stack), -jnp.inf))
        ecc = jnp.exp(cc)
        a_mat = jnp.where(strict, bcol * _mm_nt(kn, kn) * decay, 0.0)
        rhs = jnp.concatenate([bcol * vs, (bcol * ecc) * kn], axis=1)
        qk = jnp.where(causal, _mm_nt(qn, kn) * decay, 0.0)
        kd = jnp.exp(_repeat_rows(cum_last, h0, stack, chunk) - cc) * kn
        dn_groups.append((qn, ecc, a_mat, rhs, qk, kd))

    pows = [_split(grp[2]) for grp in dn_groups]
    sols = [grp[3] - _mm3_split(p, _split(grp[3])) for p, grp in zip(pows, dn_groups)]
    for level in range(n_squarings):
        if level < SOLVE_SPLIT_LEVELS:
            pows = [_split(_mm3_split(p, p)) for p in pows]
            sols = [s + _mm3_split(p, _split(s)) for p, s in zip(pows, sols)]
        else:
            pows = [(jnp.dot(p[0], p[0], preferred_element_type=F32).astype(BF16),) for p in pows]
            sols = [s + jnp.dot(p[0], s.astype(BF16), preferred_element_type=F32)
                    for p, s in zip(pows, sols)]

    conv, ss_tail = _causal_conv(ss_buf, xbc_ref[...], sscw_ref[...], chunk)
    xbc = jax.nn.silu(conv + sscb_ref[...])
    cum_ss = cum[:, DN_HEADS:DN_HEADS + SSM_HEADS]
    end_ss = cum_last[:, DN_HEADS:DN_HEADS + SSM_HEADS]
    ecum_ss = jnp.exp(cum_ss)
    tok_w = dt * jnp.exp(end_ss - cum_ss)
    e_end_ss = jnp.exp(end_ss)
    d_skip = ssd_ref[...]
    y_groups = []
    for g in range(SSM_GROUPS):
        g0 = g * HEADS_PER_GROUP
        b_g = xbc[:, D_SSM + g * SSM_STATE:D_SSM + (g + 1) * SSM_STATE]
        c_g = xbc[:, D_SSM + SSM_BC + g * SSM_STATE:D_SSM + SSM_BC + (g + 1) * SSM_STATE]
        cb = _mm_nt(c_g, b_g)
        cb_wide = jnp.concatenate([cb] * ss_stack, axis=1)
        cb_tiled = jnp.concatenate([cb_wide] * ss_stack, axis=0)
        intra = []
        for h0 in range(g0, g0 + HEADS_PER_GROUP, ss_stack):
            cc = _stack_lanes(cum_ss, h0, ss_stack, 1)
            lmat = jnp.exp(jnp.where(causal_ss, cc - _stack_rows(cum_t, DN_HEADS + h0, ss_stack),
                                     -jnp.inf))
            xdt = _stack_lanes(xbc, h0, ss_stack, SSM_HEAD) * _stack_lanes(dt, h0, ss_stack, 1)
            intra.append(_unstack_lanes(_mm(lmat * cb_tiled, xdt), ss_stack, chunk))
        x_g = xbc[:, g0 * SSM_HEAD:(g0 + HEADS_PER_GROUP) * SSM_HEAD]
        ht = ht_ref[g]
        y_g = (jnp.concatenate(intra, axis=1)
               + _spread_heads(ecum_ss, g0) * _mm(c_g, ht)
               + _spread_heads(d_skip, g0) * x_g)
        y_groups.append(y_g)
        xw = x_g * _spread_heads(tok_w, g0)
        ht_ref[g] = _spread_heads(e_end_ss, g0) * ht + _mm_tn(b_g, xw)
    y = jnp.concatenate(y_groups, axis=1) * jax.nn.silu(sz_ref[...])
    y = y * lax.rsqrt(jnp.mean(y * y, axis=-1, keepdims=True) + 1e-6) * ssg_ref[...]
    om_ref[:, DN_QK:D_MIX] = y

    dn_gain = dng_ref[...]
    e_end_dn = jnp.exp(cum_last)
    o_groups = []
    n_rows = stack * chunk
    for g, (qn, ecc, _, _, qk, kd) in enumerate(dn_groups):
        h0 = g * stack
        s = jnp.concatenate([dnr_ref[0, h0 + i] for i in range(stack)], axis=0)
        u = sols[g][:, :DN_HEAD]
        w = sols[g][:, DN_HEAD:]
        wq = _mm(jnp.concatenate([_block_diag(w, stack, own_head),
                                  _block_diag(qn, stack, own_head)], axis=0), s)
        delta = u - wq[:n_rows]
        o = ecc * wq[n_rows:] + _mm(qk, delta)
        s_new = (_repeat_rows(e_end_dn, h0, stack, DN_HEAD) * s
                 + _mm_tn(_block_diag(kd, stack, own_head), delta))
        for i in range(stack):
            dnr_ref[0, h0 + i] = s_new[i * DN_HEAD:(i + 1) * DN_HEAD, :]
        on = o * lax.rsqrt(jnp.mean(o * o, axis=-1, keepdims=True) + 1e-6) * dn_gain
        o_groups.append(_unstack_lanes(on, stack, chunk))
    om_ref[:, 0:DN_QK] = jnp.concatenate(o_groups, axis=1) * jax.nn.silu(z_ref[...])

    @pl.when(c == n_chunks - 1)
    def _finish():
        dnc_ref[0] = dn_tail
        ssc_ref[0] = ss_tail
        for g in range(SSM_GROUPS):
            rows = ht_ref[g].T
            for r in range(HEADS_PER_GROUP):
                ssr_ref[0, g * HEADS_PER_GROUP + r] = rows[r * SSM_HEAD:(r + 1) * SSM_HEAD, :]


def _mixer(proj, init, weights, n_seq, length, chunk, shared_init):
    n_chunks = length // chunk
    tok = lambda b, c: (b * n_chunks + c, 0)
    if shared_init:
        st3 = lambda b, c: (0, 0, 0)
        st4 = lambda b, c: (0, 0, 0, 0)
    else:
        st3 = lambda b, c: (b, 0, 0)
        st4 = lambda b, c: (b, 0, 0, 0)
    out3 = lambda b, c: (b, 0, 0)
    out4 = lambda b, c: (b, 0, 0, 0)
    fixed = lambda b, c: (0, 0)
    state_shapes = ((CONV_K - 1, DN_QKV), (DN_HEADS, DN_HEAD, DN_HEAD),
                    (CONV_K - 1, SSM_XBC), (SSM_HEADS, SSM_HEAD, SSM_STATE))

    def state_spec(shape, m3, m4):
        return pl.BlockSpec((1,) + shape, m3 if len(shape) == 2 else m4)

    in_specs = [pl.BlockSpec((chunk, width), tok) for width in PROJ_SPLITS]
    in_specs += [state_spec(s, st3, st4) for s in state_shapes]
    in_specs += [pl.BlockSpec(w.shape, fixed) for w in weights]
    out_specs = [pl.BlockSpec((chunk, D_MIX), tok)]
    out_specs += [state_spec(s, out3, out4) for s in state_shapes]
    out_shape = [jax.ShapeDtypeStruct((n_seq * length, D_MIX), F32)]
    out_shape += [jax.ShapeDtypeStruct((n_seq,) + s, F32) for s in state_shapes]
    outs = pl.pallas_call(
        functools.partial(_mixer_kernel, chunk=chunk, n_chunks=n_chunks),
        grid=(n_seq, n_chunks),
        in_specs=in_specs,
        out_specs=out_specs,
        out_shape=out_shape,
        scratch_shapes=[pltpu.VMEM((CONV_PAD + chunk, DN_QKV), F32),
                        pltpu.VMEM((CONV_PAD + chunk, SSM_XBC), F32),
                        pltpu.VMEM((SSM_GROUPS, SSM_STATE, HEADS_PER_GROUP * SSM_HEAD), F32)],
        compiler_params=pltpu.CompilerParams(dimension_semantics=("parallel", "arbitrary"),
                                             vmem_limit_bytes=VMEM_LIMIT),
        name=f"mixer_c{chunk}",
    )(*proj, *init, *weights)
    return outs[0], tuple(outs[1:])


def _out_ln1_kernel(x_ref, om_ref, w_ref, g0_ref, b0_ref, g1_ref, b1_ref, h_ref):
    h0 = _layer_norm(x_ref[...], g0_ref[...], b0_ref[...])
    mix = jnp.dot(om_ref[...].astype(BF16), w_ref[...], preferred_element_type=F32)
    h_ref[...] = _layer_norm(ALPHA * h0 + mix, g1_ref[...], b1_ref[...])


def _out_ln1(x, om, w, g0, b0, g1, b1, tm):
    t = x.shape[0]
    row = lambda i: (i, 0)
    fixed = lambda i: (0, 0)
    vec = pl.BlockSpec((1, D_MODEL), fixed)
    return pl.pallas_call(
        _out_ln1_kernel,
        grid=(t // tm,),
        in_specs=[pl.BlockSpec((tm, D_MODEL), row), pl.BlockSpec((tm, D_MIX), row),
                  pl.BlockSpec((D_MIX, D_MODEL), fixed), vec, vec, vec, vec],
        out_specs=pl.BlockSpec((tm, D_MODEL), row),
        out_shape=jax.ShapeDtypeStruct((t, D_MODEL), F32),
        compiler_params=pltpu.CompilerParams(dimension_semantics=("parallel",),
                                             vmem_limit_bytes=VMEM_LIMIT),
        name="out_ln1",
    )(x, om, w, g0, b0, g1, b1)


PEER_TOKENS = 512
PEER_SUBKEYS = 4
PEER_BLOCK = PEER_SUBKEYS * N_KEYS
PEER_BLOCKS = N_KEYS // PEER_SUBKEYS
PEER_STEPS = PEER_BLOCKS // 2
PAIR_LIMITS = tuple(TOP_K // (j + 1) for j in range(TOP_K))


def _top_distinct(s):
    vals, counts = [], []
    for _ in range(TOP_K):
        m = jnp.max(s, axis=0, keepdims=True)
        eq = s == m
        counts.append(jnp.sum(jnp.where(eq, 1.0, 0.0), axis=0, keepdims=True))
        vals.append(m)
        s = jnp.where(eq, -jnp.inf, s)
    return vals, counts


def _peer_route(q1, q2, k1, k2):
    s1 = jnp.dot(k1, q1, preferred_element_type=F32)
    s2 = jnp.dot(k2, q2, preferred_element_type=F32)
    a, na = _top_distinct(s1)
    b, nb = _top_distinct(s2)
    cand, mult, gate = [], [], []
    for j in range(TOP_K):
        ea = jnp.exp(a[j] - a[0])
        for l in range(PAIR_LIMITS[j]):
            cand.append(a[j] + b[l])
            mult.append(na[j] * nb[l])
            gate.append(ea * jnp.exp(b[l] - b[0]))
    cand = jnp.concatenate(cand, axis=0)
    mult = jnp.concatenate(mult, axis=0)
    gate = jnp.concatenate(gate, axis=0)
    theta = jnp.full_like(cand[0:1, :], jnp.inf)
    for i in range(cand.shape[0]):
        ci = cand[i:i + 1, :]
        n_above = jnp.sum(jnp.where(cand > ci, mult, 0.0), axis=0, keepdims=True)
        theta = jnp.where(n_above < TOP_K, jnp.minimum(theta, ci), theta)
    zsum = jnp.sum(jnp.where(cand >= theta, mult * gate, 0.0), axis=0, keepdims=True)
    n1 = jnp.zeros_like(s1)
    rank2 = jnp.zeros_like(s2)
    for l in range(TOP_K):
        n1 = n1 + jnp.where(s1 + b[l] >= theta, 1.0, 0.0)
        rank2 = rank2 + jnp.where(b[l] > s2, 1.0, 0.0)
    w1 = jnp.exp(s1 - a[0]) / zsum
    e2 = jnp.exp(s2 - b[0])
    return n1, w1, rank2.astype(BF16), e2.astype(BF16)


def _packed_bf16(words_ref):
    return pltpu.bitcast(words_ref[...], BF16)


def _peer_gated_act(p_ref, hid_ref, first_key, n1_ref, w1_ref, rank2_ref, e2_ref):
    sqrt_half = np.float32(np.sqrt(0.5))
    tile = (N_KEYS, LANES)
    for k in range(PEER_SUBKEYS):
        i1 = first_key + k
        rows = slice(k * N_KEYS, (k + 1) * N_KEYS)
        n_rows = [n1_ref[h, pl.ds(i1, 1), :] for h in range(PEER_HEADS)]
        w_rows = [w1_ref[h, pl.ds(i1, 1), :] for h in range(PEER_HEADS)]
        for s in range(PEER_TOKENS // LANES):
            lanes = slice(s * LANES, (s + 1) * LANES)
            gate = jnp.zeros(tile, BF16)
            for h in range(PEER_HEADS):
                n_row = jnp.broadcast_to(n_rows[h][:, lanes], tile).astype(BF16)
                w_row = jnp.broadcast_to(w_rows[h][:, lanes], tile).astype(BF16)
                e2 = e2_ref[h, :, lanes]
                sel = jnp.where(rank2_ref[h, :, lanes] < n_row, e2, jnp.zeros_like(e2))
                gate = gate + sel * w_row
            hk = hid_ref[rows, lanes]
            act = 0.5 * hk * (1.0 + lax.erf(hk * sqrt_half))
            words = slice(k * (N_KEYS // 2), (k + 1) * (N_KEYS // 2))
            p_ref[words, lanes] = pltpu.bitcast(gate * act.astype(BF16), jnp.uint32)


def _peer_kernel(h_ref, wq_ref, keys_ref, u0_ref, ua_ref, ub_ref, va_ref, vb_ref, g2_ref, b2_ref,
                 y_ref, xt_ref, n1_ref, w1_ref, rank2_ref, e2_ref,
                 hid_a, hid_b, p_a, p_b, acc_ref):
    j = pl.program_id(1)
    dot = functools.partial(jnp.dot, preferred_element_type=F32)

    @pl.when(j == 0)
    def _route():
        xt = h_ref[...].T.astype(BF16)
        xt_ref[...] = xt
        qt = dot(wq_ref[...], xt).astype(BF16)
        for h in range(PEER_HEADS):
            base = h * 2 * D_HALF
            n1, w1, rank2, e2 = _peer_route(
                qt[base:base + D_HALF], qt[base + D_HALF:base + 2 * D_HALF],
                keys_ref[2 * h], keys_ref[2 * h + 1])
            n1_ref[h] = n1
            w1_ref[h] = w1
            rank2_ref[h] = rank2
            e2_ref[h] = e2
        hid_a[...] = dot(u0_ref[...], xt)
        p_b[...] = jnp.zeros_like(p_b)
        acc_ref[...] = jnp.zeros_like(acc_ref)

    @pl.when(j < PEER_STEPS)
    def _step():
        xt = xt_ref[...]
        tables = (n1_ref, w1_ref, rank2_ref, e2_ref)
        _peer_gated_act(p_a, hid_a, 2 * j * PEER_SUBKEYS, *tables)
        hid_b[...] = dot(ua_ref[...], xt)
        acc_ref[...] += dot(va_ref[...], _packed_bf16(p_b))
        _peer_gated_act(p_b, hid_b, (2 * j + 1) * PEER_SUBKEYS, *tables)
        hid_a[...] = dot(ub_ref[...], xt)
        acc_ref[...] += dot(vb_ref[...], _packed_bf16(p_a))

    @pl.when(j == PEER_STEPS)
    def _finish():
        ff = acc_ref[...] + dot(va_ref[...], _packed_bf16(p_b))
        y_ref[...] = _layer_norm(ALPHA * h_ref[...] + ff.T, g2_ref[...], b2_ref[...])


def _peer(h, wq_t, keys, u, v_t, g2, b2):
    t = h.shape[0]
    last = PEER_BLOCKS - 1
    tok = lambda i, j: (i, 0)
    fixed2 = lambda i, j: (0, 0)
    u_spec = lambda index: pl.BlockSpec((PEER_BLOCK, D_MODEL), index)
    v_spec = lambda index: pl.BlockSpec((D_MODEL, PEER_BLOCK), index)
    head_tile = (PEER_HEADS, N_KEYS, PEER_TOKENS)
    return pl.pallas_call(
        _peer_kernel,
        grid=(t // PEER_TOKENS, PEER_STEPS + 1),
        in_specs=[pl.BlockSpec((PEER_TOKENS, D_MODEL), tok),
                  pl.BlockSpec(wq_t.shape, fixed2),
                  pl.BlockSpec(keys.shape, lambda i, j: (0, 0, 0)),
                  u_spec(fixed2),
                  u_spec(lambda i, j: (jnp.minimum(2 * j + 1, last), 0)),
                  u_spec(lambda i, j: (jnp.minimum(2 * j + 2, last), 0)),
                  v_spec(lambda i, j: (0, jnp.maximum(2 * j - 1, 0))),
                  v_spec(lambda i, j: (0, jnp.minimum(2 * j, last))),
                  pl.BlockSpec((1, D_MODEL), fixed2),
                  pl.BlockSpec((1, D_MODEL), fixed2)],
        out_specs=pl.BlockSpec((PEER_TOKENS, D_MODEL), tok),
        out_shape=jax.ShapeDtypeStruct((t, D_MODEL), F32),
        scratch_shapes=[pltpu.VMEM((D_MODEL, PEER_TOKENS), BF16),
                        pltpu.VMEM(head_tile, F32), pltpu.VMEM(head_tile, F32),
                        pltpu.VMEM(head_tile, BF16), pltpu.VMEM(head_tile, BF16),
                        pltpu.VMEM((PEER_BLOCK, PEER_TOKENS), F32),
                        pltpu.VMEM((PEER_BLOCK, PEER_TOKENS), F32),
                        pltpu.VMEM((PEER_BLOCK // 2, PEER_TOKENS), jnp.uint32),
                        pltpu.VMEM((PEER_BLOCK // 2, PEER_TOKENS), jnp.uint32),
                        pltpu.VMEM((D_MODEL, PEER_TOKENS), F32)],
        compiler_params=pltpu.CompilerParams(dimension_semantics=("parallel", "arbitrary"),
                                             vmem_limit_bytes=VMEM_LIMIT),
        name="peer",
    )(h, wq_t, keys, u, u, u, v_t, v_t, g2, b2)


def kernel(x_prompt, x_sample, state_dn_conv, state_dn_rec, state_ssm_conv, state_ssm_rec, meta_tokens, ln_in_g, ln_in_b, w_in, dn_conv_w, dn_A_log, dn_dt_bias, dn_norm_g, ssm_conv_w, ssm_conv_b, ssm_A_log, ssm_dt_bias, ssm_D, ssm_norm_g, w_out, ln1_g, ln1_b, peer_w_q, peer_keys, peer_u, peer_v, ln2_g, ln2_b):
    n_prompt, seq, _ = x_prompt.shape
    n_sample, dec_seq, _ = x_sample.shape
    row = lambda v: v.reshape(1, -1)

    w = w_in[0]
    o_z = DN_QKV
    o_b = o_z + DN_QK
    o_a = o_b + DN_HEADS
    o_sz = o_a + DN_HEADS
    o_xbc = o_sz + D_SSM
    o_dt = o_xbc + SSM_XBC
    gates = jnp.concatenate(
        [w[:, o_b:o_a], w[:, o_a:o_sz], w[:, o_dt:],
         jnp.zeros((D_MODEL, GATE_LANES - 2 * DN_HEADS - SSM_HEADS), w.dtype)], axis=1)
    w_proj = jnp.concatenate(
        [w[:, :o_z], w[:, o_z:o_b], w[:, o_sz:o_xbc], w[:, o_xbc:o_dt], gates],
        axis=1).astype(BF16)
    w_o = w_out[0].astype(BF16)
    wq_t = peer_w_q[0].T.astype(BF16)
    keys = peer_keys[0].reshape(PEER_HEADS * 2, N_KEYS, D_HALF).astype(BF16)
    u_tab = peer_u[0].astype(BF16)
    v_t = peer_v[0].T.astype(BF16)

    g_in, b_in = row(ln_in_g), row(ln_in_b)
    mixer_w = (dn_conv_w[0], row(dn_A_log[0]), row(dn_dt_bias[0]), row(dn_norm_g[0]),
               ssm_conv_w[0], row(ssm_conv_b[0]), row(ssm_A_log[0]), row(ssm_dt_bias[0]),
               row(ssm_D[0]), row(ssm_norm_g[0]))

    def trunk(x, init, n_seq, length, chunk, shared_init):
        proj = _ln_proj(x, g_in, b_in, w_proj, tm=min(256, x.shape[0]))
        om, states = _mixer(proj, init, mixer_w, n_seq, length, chunk, shared_init)
        return om, states

    zero_init = (jnp.zeros((1, CONV_K - 1, DN_QKV), F32),
                 jnp.zeros((1, DN_HEADS, DN_HEAD, DN_HEAD), F32),
                 jnp.zeros((1, CONV_K - 1, SSM_XBC), F32),
                 jnp.zeros((1, SSM_HEADS, SSM_HEAD, SSM_STATE), F32))
    _, meta_state = trunk(meta_tokens, zero_init, 1, N_META, N_META, False)

    xp = x_prompt.reshape(n_prompt * seq, D_MODEL)
    om_p, st_p = trunk(xp, meta_state, n_prompt, seq, CHUNK, True)
    xs = x_sample.reshape(n_sample * dec_seq, D_MODEL)
    init_s = (state_dn_conv[0], state_dn_rec[0], state_ssm_conv[0], state_ssm_rec[0])
    om_s, st_s = trunk(xs, init_s, n_sample, dec_seq, dec_seq, False)

    def tail(x, om):
        h1 = _out_ln1(x, om, w_o, g_in, b_in, row(ln1_g[0]), row(ln1_b[0]), tm=256)
        return _peer(h1, wq_t, keys, u_tab, v_t, row(ln2_g[0]), row(ln2_b[0]))

    y_p = tail(xp, om_p).reshape(n_prompt, seq, D_MODEL)
    y_s = tail(xs, om_s).reshape(n_sample, dec_seq, D_MODEL)
    return (y_p, y_s) + tuple(s[None] for s in st_p) + tuple(s[None] for s in st_s)
```

```python
import functools
import math

import numpy as np
import jax
import jax.numpy as jnp
from jax import lax
from jax.experimental import pallas as pl
from jax.experimental.pallas import tpu as pltpu

F32 = jnp.float32
BF16 = jnp.bfloat16

D_MODEL = 1024
N_META = 16
DN_HEADS = 16
DN_HEAD = 64
DN_QK = DN_HEADS * DN_HEAD
DN_QKV = 3 * DN_QK
SSM_HEADS = 16
SSM_HEAD = 64
SSM_GROUPS = 2
SSM_STATE = 128
D_SSM = SSM_HEADS * SSM_HEAD
SSM_BC = SSM_GROUPS * SSM_STATE
SSM_XBC = D_SSM + 2 * SSM_BC
HEADS_PER_GROUP = SSM_HEADS // SSM_GROUPS
D_MIX = DN_QK + D_SSM
CONV_K = 4
CHUNK = 64
PEER_HEADS = 8
N_KEYS = 128
D_HALF = 128
TOP_K = 16
ALPHA = 2.0 ** 0.25
GATE_LANES = 128
LANES = 128
MXU_DIM = 256
VMEM_LIMIT = 56 * 1024 * 1024


def _layer_norm(x, g, b, eps=1e-5):
    mu = jnp.mean(x, axis=-1, keepdims=True)
    xc = x - mu
    var = jnp.mean(xc * xc, axis=-1, keepdims=True)
    return xc * lax.rsqrt(var + eps) * g + b


def _mm(a, b):
    return jnp.dot(a.astype(BF16), b.astype(BF16), preferred_element_type=F32)


def _mm_nt(a, b):
    return lax.dot_general(a.astype(BF16), b.astype(BF16), (((1,), (1,)), ((), ())),
                           preferred_element_type=F32)


def _mm_tn(a, b):
    return lax.dot_general(a.astype(BF16), b.astype(BF16), (((0,), (0,)), ((), ())),
                           preferred_element_type=F32)


def _split(x):
    hi = x.astype(BF16)
    return hi, (x - hi.astype(F32)).astype(BF16)


def _mm2_split(a, b):
    dot = functools.partial(jnp.dot, preferred_element_type=F32)
    return dot(a[0], b[0]) + dot(a[0], b[1])


def _mm_01(m01, x):
    hi = x.astype(BF16)
    r = x - hi.astype(F32)
    mid = r.astype(BF16)
    lo = (r - mid.astype(F32)).astype(BF16)
    dot = functools.partial(jnp.dot, preferred_element_type=F32)
    return dot(m01, hi) + dot(m01, mid) + dot(m01, lo)


def _softplus(x):
    return jnp.maximum(x, 0.0) + jnp.log1p(jnp.exp(-jnp.abs(x)))


PROJ_SPLITS = (DN_QKV, DN_QK, D_SSM, SSM_XBC, GATE_LANES)
PROJ_COLS = sum(PROJ_SPLITS)


def _ln_proj_kernel(x_ref, g_ref, b_ref, w_ref, qkv_ref, z_ref, sz_ref, xbc_ref, gate_ref):
    xn = _layer_norm(x_ref[...], g_ref[...], b_ref[...]).astype(BF16)
    start = 0
    for out_ref, width in zip((qkv_ref, z_ref, sz_ref, xbc_ref, gate_ref), PROJ_SPLITS):
        out_ref[...] = jnp.dot(xn, w_ref[:, start:start + width], preferred_element_type=F32)
        start += width


def _ln_proj(x, g, b, w, tm):
    t = x.shape[0]
    row = lambda i: (i, 0)
    fixed = lambda i: (0, 0)
    return pl.pallas_call(
        _ln_proj_kernel,
        grid=(t // tm,),
        in_specs=[pl.BlockSpec((tm, D_MODEL), row),
                  pl.BlockSpec((1, D_MODEL), fixed),
                  pl.BlockSpec((1, D_MODEL), fixed),
                  pl.BlockSpec((D_MODEL, PROJ_COLS), fixed, pipeline_mode=pl.Buffered(1))],
        out_specs=[pl.BlockSpec((tm, width), row) for width in PROJ_SPLITS],
        out_shape=[jax.ShapeDtypeStruct((t, width), F32) for width in PROJ_SPLITS],
        compiler_params=pltpu.CompilerParams(dimension_semantics=("parallel",),
                                             vmem_limit_bytes=VMEM_LIMIT),
        name="ln_proj",
    )(x, g, b, w)


CONV_PAD = 8
SOLVE_SPLIT_LEVELS = 3


def _causal_conv(buf_ref, x, w, chunk):
    buf_ref[CONV_PAD:CONV_PAD + chunk, :] = x
    first = CONV_PAD - (CONV_K - 1)
    y = w[0:1] * buf_ref[first:first + chunk, :]
    for i in range(1, CONV_K):
        y = y + w[i:i + 1] * buf_ref[first + i:first + i + chunk, :]
    tail = buf_ref[first + chunk:CONV_PAD + chunk, :]
    buf_ref[first:CONV_PAD, :] = tail
    return y, tail


def _stack_lanes(x, first, count, width):
    return jnp.concatenate(
        [x[:, (first + i) * width:(first + i + 1) * width] for i in range(count)], axis=0)


def _unstack_lanes(xs, count, chunk):
    return jnp.concatenate([xs[i * chunk:(i + 1) * chunk, :] for i in range(count)], axis=1)


def _stack_rows(xt, first, count):
    return jnp.concatenate([xt[first + i:first + i + 1, :] for i in range(count)], axis=1)


def _repeat_rows(v, first, count, reps):
    return jnp.concatenate(
        [jnp.broadcast_to(v[:, first + i:first + i + 1], (reps, 1)) for i in range(count)], axis=0)


def _spread_heads(a, first):
    return jnp.concatenate(
        [jnp.broadcast_to(a[:, first + r:first + r + 1], (a.shape[0], SSM_HEAD))
         for r in range(HEADS_PER_GROUP)], axis=1)


def _block_diag(x, count, own_block):
    return jnp.where(own_block, jnp.concatenate([x] * count, axis=1), 0.0)


def _transpose_rows(x, chunk):
    if chunk < LANES:
        x = jnp.concatenate([x, jnp.zeros((LANES - chunk, LANES), F32)], axis=0)
    return x.T[:, :chunk]


def _mixer_kernel(qkv_ref, z_ref, sz_ref, xbc_ref, gate_ref,
                  dnc0_ref, dnr0_ref, ssc0_ref, ssr0_ref,
                  dncw_ref, dna_ref, dndt_ref, dng_ref,
                  sscw_ref, sscb_ref, ssa_ref, ssdt_ref, ssd_ref, ssg_ref,
                  om_ref, dnc_ref, dnr_ref, ssc_ref, ssr_ref,
                  dn_buf, ss_buf, ht_ref, *, chunk, n_chunks):
    c = pl.program_id(1)

    @pl.when(c == 0)
    def _init():
        dn_buf[CONV_PAD - (CONV_K - 1):CONV_PAD, :] = dnc0_ref[0]
        ss_buf[CONV_PAD - (CONV_K - 1):CONV_PAD, :] = ssc0_ref[0]
        dnr_ref[0] = dnr0_ref[0]
        for g in range(SSM_GROUPS):
            rows = jnp.concatenate(
                [ssr0_ref[0, g * HEADS_PER_GROUP + r] for r in range(HEADS_PER_GROUP)], axis=0)
            ht_ref[g] = rows.T

    stack = min(DN_HEADS, MXU_DIM // chunk)
    ss_stack = min(stack, HEADS_PER_GROUP)
    shift = int(math.log2(chunk))
    n_squarings = shift - 1

    def block_masks(n):
        r_i = lax.broadcasted_iota(jnp.int32, (n, n), 0)
        c_i = lax.broadcasted_iota(jnp.int32, (n, n), 1)
        same = (r_i >> shift) == (c_i >> shift)
        return same & (r_i >= c_i), same & (r_i > c_i)

    causal, strict = block_masks(stack * chunk)
    causal_ss = causal if ss_stack == stack else block_masks(ss_stack * chunk)[0]
    head_of_row = lax.broadcasted_iota(jnp.int32, (stack * chunk, stack * DN_HEAD), 0) >> shift
    head_of_lane = (lax.broadcasted_iota(jnp.int32, (stack * chunk, stack * DN_HEAD), 1)
                    >> int(math.log2(DN_HEAD)))
    own_head = head_of_row == head_of_lane
    tril01 = (lax.broadcasted_iota(jnp.int32, (chunk, chunk), 0)
              >= lax.broadcasted_iota(jnp.int32, (chunk, chunk), 1)).astype(BF16)

    gate = gate_ref[...]
    beta = jax.nn.sigmoid(gate[:, 0:DN_HEADS])
    g_dn = -jnp.exp(dna_ref[...]) * _softplus(gate[:, DN_HEADS:2 * DN_HEADS] + dndt_ref[...])
    dt = _softplus(gate[:, 2 * DN_HEADS:2 * DN_HEADS + SSM_HEADS] + ssdt_ref[...])
    da = dt * (-jnp.exp(ssa_ref[...]))
    logdec = jnp.concatenate(
        [g_dn, da, jnp.zeros((chunk, LANES - DN_HEADS - SSM_HEADS), F32)], axis=1)
    cum = _mm_01(tril01, logdec)
    cum_t = _transpose_rows(cum, chunk)
    cum_last = cum[chunk - 1:chunk, :]

    conv, dn_tail = _causal_conv(dn_buf, qkv_ref[...], dncw_ref[...], chunk)
    qkv = jax.nn.silu(conv)
    dn_groups = []
    for g in range(DN_HEADS // stack):
        h0 = g * stack
        qs = _stack_lanes(qkv, h0, stack, DN_HEAD)
        ks = _stack_lanes(qkv, DN_HEADS + h0, stack, DN_HEAD)
        vs = _stack_lanes(qkv, 2 * DN_HEADS + h0, stack, DN_HEAD)
        qn = qs * lax.rsqrt(jnp.sum(qs * qs, axis=-1, keepdims=True) + 1e-6) * (DN_HEAD ** -0.5)
        kn = ks * lax.rsqrt(jnp.sum(ks * ks, axis=-1, keepdims=True) + 1e-6)
        cc = _stack_lanes(cum, h0, stack, 1)
        bcol = _stack_lanes(beta, h0, stack, 1)
        decay = jnp.exp(jnp.where(causal, cc - _stack_rows(cum_t, h0, stack), -jnp.inf))
        ecc = jnp.exp(cc)
        a_mat = jnp.where(strict, bcol * _mm_nt(kn, kn) * decay, 0.0)
        rhs = jnp.concatenate([bcol * vs, (bcol * ecc) * kn], axis=1)
        qk = jnp.where(causal, _mm_nt(qn, kn) * decay, 0.0)
        kd = jnp.exp(_repeat_rows(cum_last, h0, stack, chunk) - cc) * kn
        dn_groups.append((qn, ecc, a_mat, rhs, qk, kd))

    pows = [_split(grp[2]) for grp in dn_groups]
    sols = [grp[3] - _mm2_split(p, _split(grp[3])) for p, grp in zip(pows, dn_groups)]
    for level in range(n_squarings):
        if level < SOLVE_SPLIT_LEVELS:
            pows = [_split(_mm2_split(p, p)) for p in pows]
            sols = [s + _mm2_split(p, _split(s)) for p, s in zip(pows, sols)]
        else:
            pows = [(jnp.dot(p[0], p[0], preferred_element_type=F32).astype(BF16),) for p in pows]
            sols = [s + jnp.dot(p[0], s.astype(BF16), preferred_element_type=F32)
                    for p, s in zip(pows, sols)]

    conv, ss_tail = _causal_conv(ss_buf, xbc_ref[...], sscw_ref[...], chunk)
    xbc = jax.nn.silu(conv + sscb_ref[...])
    cum_ss = cum[:, DN_HEADS:DN_HEADS + SSM_HEADS]
    end_ss = cum_last[:, DN_HEADS:DN_HEADS + SSM_HEADS]
    ecum_ss = jnp.exp(cum_ss)
    tok_w = dt * jnp.exp(end_ss - cum_ss)
    e_end_ss = jnp.exp(end_ss)
    d_skip = ssd_ref[...]
    y_groups = []
    for g in range(SSM_GROUPS):
        g0 = g * HEADS_PER_GROUP
        b_g = xbc[:, D_SSM + g * SSM_STATE:D_SSM + (g + 1) * SSM_STATE]
        c_g = xbc[:, D_SSM + SSM_BC + g * SSM_STATE:D_SSM + SSM_BC + (g + 1) * SSM_STATE]
        cb = _mm_nt(c_g, b_g)
        cb_wide = jnp.concatenate([cb] * ss_stack, axis=1)
        cb_tiled = jnp.concatenate([cb_wide] * ss_stack, axis=0)
        intra = []
        for h0 in range(g0, g0 + HEADS_PER_GROUP, ss_stack):
            cc = _stack_lanes(cum_ss, h0, ss_stack, 1)
            lmat = jnp.exp(jnp.where(causal_ss, cc - _stack_rows(cum_t, DN_HEADS + h0, ss_stack),
                                     -jnp.inf))
            xdt = _stack_lanes(xbc, h0, ss_stack, SSM_HEAD) * _stack_lanes(dt, h0, ss_stack, 1)
            intra.append(_unstack_lanes(_mm(lmat * cb_tiled, xdt), ss_stack, chunk))
        x_g = xbc[:, g0 * SSM_HEAD:(g0 + HEADS_PER_GROUP) * SSM_HEAD]
        ht = ht_ref[g]
        y_g = (jnp.concatenate(intra, axis=1)
               + _spread_heads(ecum_ss, g0) * _mm(c_g, ht)
               + _spread_heads(d_skip, g0) * x_g)
        y_groups.append(y_g)
        xw = x_g * _spread_heads(tok_w, g0)
        ht_ref[g] = _spread_heads(e_end_ss, g0) * ht + _mm_tn(b_g, xw)
    y = jnp.concatenate(y_groups, axis=1) * jax.nn.silu(sz_ref[...])
    y = y * lax.rsqrt(jnp.mean(y * y, axis=-1, keepdims=True) + 1e-6) * ssg_ref[...]
    om_ref[:, DN_QK:D_MIX] = y

    dn_gain = dng_ref[...]
    e_end_dn = jnp.exp(cum_last)
    o_groups = []
    n_rows = stack * chunk
    for g, (qn, ecc, _, _, qk, kd) in enumerate(dn_groups):
        h0 = g * stack
        s = jnp.concatenate([dnr_ref[0, h0 + i] for i in range(stack)], axis=0)
        u = sols[g][:, :DN_HEAD]
        w = sols[g][:, DN_HEAD:]
        wq = _mm(jnp.concatenate([_block_diag(w, stack, own_head),
                                  _block_diag(qn, stack, own_head)], axis=0), s)
        delta = u - wq[:n_rows]
        o = ecc * wq[n_rows:] + _mm(qk, delta)
        s_new = (_repeat_rows(e_end_dn, h0, stack, DN_HEAD) * s
                 + _mm_tn(_block_diag(kd, stack, own_head), delta))
        for i in range(stack):
            dnr_ref[0, h0 + i] = s_new[i * DN_HEAD:(i + 1) * DN_HEAD, :]
        on = o * lax.rsqrt(jnp.mean(o * o, axis=-1, keepdims=True) + 1e-6) * dn_gain
        o_groups.append(_unstack_lanes(on, stack, chunk))
    om_ref[:, 0:DN_QK] = jnp.concatenate(o_groups, axis=1) * jax.nn.silu(z_ref[...])

    @pl.when(c == n_chunks - 1)
    def _finish():
        dnc_ref[0] = dn_tail
        ssc_ref[0] = ss_tail
        for g in range(SSM_GROUPS):
            rows = ht_ref[g].T
            for r in range(HEADS_PER_GROUP):
                ssr_ref[0, g * HEADS_PER_GROUP + r] = rows[r * SSM_HEAD:(r + 1) * SSM_HEAD, :]


def _mixer(proj, init, weights, n_seq, length, chunk, shared_init):
    n_chunks = length // chunk
    tok = lambda b, c: (b * n_chunks + c, 0)
    if shared_init:
        st3 = lambda b, c: (0, 0, 0)
        st4 = lambda b, c: (0, 0, 0, 0)
    else:
        st3 = lambda b, c: (b, 0, 0)
        st4 = lambda b, c: (b, 0, 0, 0)
    out3 = lambda b, c: (b, 0, 0)
    out4 = lambda b, c: (b, 0, 0, 0)
    fixed = lambda b, c: (0, 0)
    state_shapes = ((CONV_K - 1, DN_QKV), (DN_HEADS, DN_HEAD, DN_HEAD),
                    (CONV_K - 1, SSM_XBC), (SSM_HEADS, SSM_HEAD, SSM_STATE))

    def state_spec(shape, m3, m4):
        return pl.BlockSpec((1,) + shape, m3 if len(shape) == 2 else m4)

    in_specs = [pl.BlockSpec((chunk, width), tok) for width in PROJ_SPLITS]
    in_specs += [state_spec(s, st3, st4) for s in state_shapes]
    in_specs += [pl.BlockSpec(w.shape, fixed) for w in weights]
    out_specs = [pl.BlockSpec((chunk, D_MIX), tok)]
    out_specs += [state_spec(s, out3, out4) for s in state_shapes]
    out_shape = [jax.ShapeDtypeStruct((n_seq * length, D_MIX), F32)]
    out_shape += [jax.ShapeDtypeStruct((n_seq,) + s, F32) for s in state_shapes]
    outs = pl.pallas_call(
        functools.partial(_mixer_kernel, chunk=chunk, n_chunks=n_chunks),
        grid=(n_seq, n_chunks),
        in_specs=in_specs,
        out_specs=out_specs,
        out_shape=out_shape,
        scratch_shapes=[pltpu.VMEM((CONV_PAD + chunk, DN_QKV), F32),
                        pltpu.VMEM((CONV_PAD + chunk, SSM_XBC), F32),
                        pltpu.VMEM((SSM_GROUPS, SSM_STATE, HEADS_PER_GROUP * SSM_HEAD), F32)],
        compiler_params=pltpu.CompilerParams(dimension_semantics=("parallel", "arbitrary"),
                                             vmem_limit_bytes=VMEM_LIMIT),
        name=f"mixer_c{chunk}",
    )(*proj, *init, *weights)
    return outs[0], tuple(outs[1:])


def _out_ln1_kernel(x_ref, om_ref, w_ref, g0_ref, b0_ref, g1_ref, b1_ref, h_ref):
    h0 = _layer_norm(x_ref[...], g0_ref[...], b0_ref[...])
    mix = jnp.dot(om_ref[...].astype(BF16), w_ref[...], preferred_element_type=F32)
    h_ref[...] = _layer_norm(ALPHA * h0 + mix, g1_ref[...], b1_ref[...])


def _out_ln1(x, om, w, g0, b0, g1, b1, tm):
    t = x.shape[0]
    row = lambda i: (i, 0)
    fixed = lambda i: (0, 0)
    vec = pl.BlockSpec((1, D_MODEL), fixed)
    return pl.pallas_call(
        _out_ln1_kernel,
        grid=(t // tm,),
        in_specs=[pl.BlockSpec((tm, D_MODEL), row), pl.BlockSpec((tm, D_MIX), row),
                  pl.BlockSpec((D_MIX, D_MODEL), fixed), vec, vec, vec, vec],
        out_specs=pl.BlockSpec((tm, D_MODEL), row),
        out_shape=jax.ShapeDtypeStruct((t, D_MODEL), F32),
        compiler_params=pltpu.CompilerParams(dimension_semantics=("parallel",),
                                             vmem_limit_bytes=VMEM_LIMIT),
        name="out_ln1",
    )(x, om, w, g0, b0, g1, b1)


PEER_TOKENS = 512
PEER_SUBKEYS = 4
PEER_BLOCK = PEER_SUBKEYS * N_KEYS
PEER_BLOCKS = N_KEYS // PEER_SUBKEYS
PEER_STEPS = PEER_BLOCKS // 2
PAIR_LIMITS = tuple(TOP_K // (j + 1) for j in range(TOP_K))


def _top_distinct(s):
    vals, counts = [], []
    for _ in range(TOP_K):
        m = jnp.max(s, axis=0, keepdims=True)
        eq = s == m
        counts.append(jnp.sum(jnp.where(eq, 1.0, 0.0), axis=0, keepdims=True))
        vals.append(m)
        s = jnp.where(eq, -jnp.inf, s)
    return vals, counts


def _peer_route(q1, q2, k1, k2):
    s1 = jnp.dot(k1, q1, preferred_element_type=F32)
    s2 = jnp.dot(k2, q2, preferred_element_type=F32)
    a, na = _top_distinct(s1)
    b, nb = _top_distinct(s2)
    cand, mult, gate = [], [], []
    for j in range(TOP_K):
        ea = jnp.exp(a[j] - a[0])
        for l in range(PAIR_LIMITS[j]):
            cand.append(a[j] + b[l])
            mult.append(na[j] * nb[l])
            gate.append(ea * jnp.exp(b[l] - b[0]))
    cand = jnp.concatenate(cand, axis=0)
    mult = jnp.concatenate(mult, axis=0)
    gate = jnp.concatenate(gate, axis=0)
    theta = jnp.full_like(cand[0:1, :], jnp.inf)
    for i in range(cand.shape[0]):
        ci = cand[i:i + 1, :]
        n_above = jnp.sum(jnp.where(cand > ci, mult, 0.0), axis=0, keepdims=True)
        theta = jnp.where(n_above < TOP_K, jnp.minimum(theta, ci), theta)
    zsum = jnp.sum(jnp.where(cand >= theta, mult * gate, 0.0), axis=0, keepdims=True)
    n1 = jnp.zeros_like(s1)
    rank2 = jnp.zeros_like(s2)
    for l in range(TOP_K):
        n1 = n1 + jnp.where(s1 + b[l] >= theta, 1.0, 0.0)
        rank2 = rank2 + jnp.where(b[l] > s2, 1.0, 0.0)
    w1 = jnp.exp(s1 - a[0]) / zsum
    e2 = jnp.exp(s2 - b[0])
    return n1, w1, rank2.astype(BF16), e2.astype(BF16)


def _packed_bf16(words_ref):
    return pltpu.bitcast(words_ref[...], BF16)


def _peer_gated_act(p_ref, hid_ref, first_key, n1_ref, w1_ref, rank2_ref, e2_ref):
    sqrt_half = np.float32(np.sqrt(0.5))
    tile = (N_KEYS, LANES)
    for k in range(PEER_SUBKEYS):
        i1 = first_key + k
        rows = slice(k * N_KEYS, (k + 1) * N_KEYS)
        n_rows = [n1_ref[h, pl.ds(i1, 1), :] for h in range(PEER_HEADS)]
        w_rows = [w1_ref[h, pl.ds(i1, 1), :] for h in range(PEER_HEADS)]
        for s in range(PEER_TOKENS // LANES):
            lanes = slice(s * LANES, (s + 1) * LANES)
            gate = jnp.zeros(tile, BF16)
            for h in range(PEER_HEADS):
                n_row = jnp.broadcast_to(n_rows[h][:, lanes], tile).astype(BF16)
                w_row = jnp.broadcast_to(w_rows[h][:, lanes], tile).astype(BF16)
                e2 = e2_ref[h, :, lanes]
                sel = jnp.where(rank2_ref[h, :, lanes] < n_row, e2, jnp.zeros_like(e2))
                gate = gate + sel * w_row
            hk = hid_ref[rows, lanes]
            act = 0.5 * hk * (1.0 + lax.erf(hk * sqrt_half))
            words = slice(k * (N_KEYS // 2), (k + 1) * (N_KEYS // 2))
            p_ref[words, lanes] = pltpu.bitcast(gate * act.astype(BF16), jnp.uint32)


def _peer_kernel(h_ref, wq_ref, keys_ref, u0_ref, ua_ref, ub_ref, va_ref, vb_ref, g2_ref, b2_ref,
                 y_ref, xt_ref, qt_ref, n1_ref, w1_ref, rank2_ref, e2_ref,
                 hid_a, hid_b, p_a, p_b, acc_ref):
    j = pl.program_id(1)
    dot = functools.partial(jnp.dot, preferred_element_type=F32)

    @pl.when(j == 0)
    def _route():
        xt = h_ref[...].T.astype(BF16)
        xt_ref[...] = xt
        qt_ref[...] = dot(wq_ref[...], xt)

        def route_head(h, carry):
            base = pl.multiple_of(h * (2 * D_HALF), 2 * D_HALF)
            n1, w1, rank2, e2 = _peer_route(
                qt_ref[pl.ds(base, D_HALF), :].astype(BF16),
                qt_ref[pl.ds(base + D_HALF, D_HALF), :].astype(BF16),
                keys_ref[2 * h], keys_ref[2 * h + 1])
            n1_ref[h] = n1
            w1_ref[h] = w1
            rank2_ref[h] = rank2
            e2_ref[h] = e2
            return carry

        lax.fori_loop(0, PEER_HEADS, route_head, 0)
        hid_a[...] = dot(u0_ref[...], xt)
        p_b[...] = jnp.zeros_like(p_b)
        acc_ref[...] = jnp.zeros_like(acc_ref)

    @pl.when(j < PEER_STEPS)
    def _step():
        xt = xt_ref[...]
        tables = (n1_ref, w1_ref, rank2_ref, e2_ref)
        _peer_gated_act(p_a, hid_a, 2 * j * PEER_SUBKEYS, *tables)
        hid_b[...] = dot(ua_ref[...], xt)
        acc_ref[...] += dot(va_ref[...], _packed_bf16(p_b))
        _peer_gated_act(p_b, hid_b, (2 * j + 1) * PEER_SUBKEYS, *tables)
        hid_a[...] = dot(ub_ref[...], xt)
        acc_ref[...] += dot(vb_ref[...], _packed_bf16(p_a))

    @pl.when(j == PEER_STEPS)
    def _finish():
        ff = acc_ref[...] + dot(va_ref[...], _packed_bf16(p_b))
        y_ref[...] = _layer_norm(ALPHA * h_ref[...] + ff.T, g2_ref[...], b2_ref[...])


def _peer(h, wq_t, keys, u, v_t, g2, b2):
    t = h.shape[0]
    last = PEER_BLOCKS - 1
    tok = lambda i, j: (i, 0)
    fixed2 = lambda i, j: (0, 0)
    u_spec = lambda index: pl.BlockSpec((PEER_BLOCK, D_MODEL), index)
    v_spec = lambda index: pl.BlockSpec((D_MODEL, PEER_BLOCK), index)
    head_tile = (PEER_HEADS, N_KEYS, PEER_TOKENS)
    return pl.pallas_call(
        _peer_kernel,
        grid=(t // PEER_TOKENS, PEER_STEPS + 1),
        in_specs=[pl.BlockSpec((PEER_TOKENS, D_MODEL), tok),
                  pl.BlockSpec(wq_t.shape, fixed2),
                  pl.BlockSpec(keys.shape, lambda i, j: (0, 0, 0)),
                  u_spec(fixed2),
                  u_spec(lambda i, j: (jnp.minimum(2 * j + 1, last), 0)),
                  u_spec(lambda i, j: (jnp.minimum(2 * j + 2, last), 0)),
                  v_spec(lambda i, j: (0, jnp.maximum(2 * j - 1, 0))),
                  v_spec(lambda i, j: (0, jnp.minimum(2 * j, last))),
                  pl.BlockSpec((1, D_MODEL), fixed2),
                  pl.BlockSpec((1, D_MODEL), fixed2)],
        out_specs=pl.BlockSpec((PEER_TOKENS, D_MODEL), tok),
        out_shape=jax.ShapeDtypeStruct((t, D_MODEL), F32),
        scratch_shapes=[pltpu.VMEM((D_MODEL, PEER_TOKENS), BF16),
                        pltpu.VMEM((PEER_HEADS * 2 * D_HALF, PEER_TOKENS), F32),
                        pltpu.VMEM(head_tile, F32), pltpu.VMEM(head_tile, F32),
                        pltpu.VMEM(head_tile, BF16), pltpu.VMEM(head_tile, BF16),
                        pltpu.VMEM((PEER_BLOCK, PEER_TOKENS), F32),
                        pltpu.VMEM((PEER_BLOCK, PEER_TOKENS), F32),
                        pltpu.VMEM((PEER_BLOCK // 2, PEER_TOKENS), jnp.uint32),
                        pltpu.VMEM((PEER_BLOCK // 2, PEER_TOKENS), jnp.uint32),
                        pltpu.VMEM((D_MODEL, PEER_TOKENS), F32)],
        compiler_params=pltpu.CompilerParams(dimension_semantics=("parallel", "arbitrary"),
                                             vmem_limit_bytes=VMEM_LIMIT),
        name="peer",
    )(h, wq_t, keys, u, u, u, v_t, v_t, g2, b2)


def kernel(x_prompt, x_sample, state_dn_conv, state_dn_rec, state_ssm_conv, state_ssm_rec, meta_tokens, ln_in_g, ln_in_b, w_in, dn_conv_w, dn_A_log, dn_dt_bias, dn_norm_g, ssm_conv_w, ssm_conv_b, ssm_A_log, ssm_dt_bias, ssm_D, ssm_norm_g, w_out, ln1_g, ln1_b, peer_w_q, peer_keys, peer_u, peer_v, ln2_g, ln2_b):
    n_prompt, seq, _ = x_prompt.shape
    n_sample, dec_seq, _ = x_sample.shape
    row = lambda v: v.reshape(1, -1)

    w = w_in[0]
    o_z = DN_QKV
    o_b = o_z + DN_QK
    o_a = o_b + DN_HEADS
    o_sz = o_a + DN_HEADS
    o_xbc = o_sz + D_SSM
    o_dt = o_xbc + SSM_XBC
    gates = jnp.concatenate(
        [w[:, o_b:o_a], w[:, o_a:o_sz], w[:, o_dt:],
         jnp.zeros((D_MODEL, GATE_LANES - 2 * DN_HEADS - SSM_HEADS), w.dtype)], axis=1)
    w_proj = jnp.concatenate(
        [w[:, :o_z], w[:, o_z:o_b], w[:, o_sz:o_xbc], w[:, o_xbc:o_dt], gates],
        axis=1).astype(BF16)
    w_o = w_out[0].astype(BF16)
    wq_t = peer_w_q[0].T.astype(BF16)
    keys = peer_keys[0].reshape(PEER_HEADS * 2, N_KEYS, D_HALF).astype(BF16)
    u_tab = peer_u[0].astype(BF16)
    v_t = peer_v[0].T.astype(BF16)

    g_in, b_in = row(ln_in_g), row(ln_in_b)
    mixer_w = (dn_conv_w[0], row(dn_A_log[0]), row(dn_dt_bias[0]), row(dn_norm_g[0]),
               ssm_conv_w[0], row(ssm_conv_b[0]), row(ssm_A_log[0]), row(ssm_dt_bias[0]),
               row(ssm_D[0]), row(ssm_norm_g[0]))

    def trunk(x, init, n_seq, length, chunk, shared_init):
        proj = _ln_proj(x, g_in, b_in, w_proj, tm=min(256, x.shape[0]))
        om, states = _mixer(proj, init, mixer_w, n_seq, length, chunk, shared_init)
        return om, states

    zero_init = (jnp.zeros((1, CONV_K - 1, DN_QKV), F32),
                 jnp.zeros((1, DN_HEADS, DN_HEAD, DN_HEAD), F32),
                 jnp.zeros((1, CONV_K - 1, SSM_XBC), F32),
                 jnp.zeros((1, SSM_HEADS, SSM_HEAD, SSM_STATE), F32))
    _, meta_state = trunk(meta_tokens, zero_init, 1, N_META, N_META, False)

    xp = x_prompt.reshape(n_prompt * seq, D_MODEL)
    om_p, st_p = trunk(xp, meta_state, n_prompt, seq, CHUNK, True)
    xs = x_sample.reshape(n_sample * dec_seq, D_MODEL)
    init_s = (state_dn_conv[0], state_dn_rec[0], state_ssm_conv[0], state_ssm_rec[0])
    om_s, st_s = trunk(xs, init_s, n_sample, dec_seq, dec_seq, False)

    def tail(x, om):
        h1 = _out_ln1(x, om, w_o, g_in, b_in, row(ln1_g[0]), row(ln1_b[0]), tm=256)
        return _peer(h1, wq_t, keys, u_tab, v_t, row(ln2_g[0]), row(ln2_b[0]))

    y_p = tail(xp, om_p).reshape(n_prompt, seq, D_MODEL)
    y_s = tail(xs, om_s).reshape(n_sample, dec_seq, D_MODEL)
    return (y_p, y_s) + tuple(s[None] for s in st_p) + tuple(s[None] for s in st_s)
```

```python
import functools
import math

import numpy as np
import jax
import jax.numpy as jnp
from jax import lax
from jax.experimental import pallas as pl
from jax.experimental.pallas import tpu as pltpu

F32 = jnp.float32
BF16 = jnp.bfloat16

D_MODEL = 1024
N_META = 16
DN_HEADS = 16
DN_HEAD = 64
DN_QK = DN_HEADS * DN_HEAD
DN_QKV = 3 * DN_QK
SSM_HEADS = 16
SSM_HEAD = 64
SSM_GROUPS = 2
SSM_STATE = 128
D_SSM = SSM_HEADS * SSM_HEAD
SSM_BC = SSM_GROUPS * SSM_STATE
SSM_XBC = D_SSM + 2 * SSM_BC
HEADS_PER_GROUP = SSM_HEADS // SSM_GROUPS
D_MIX = DN_QK + D_SSM
CONV_K = 4
CHUNK = 64
PEER_HEADS = 8
N_KEYS = 128
D_HALF = 128
TOP_K = 16
ALPHA = 2.0 ** 0.25
GATE_LANES = 128
LANES = 128
MXU_DIM = 256
VMEM_LIMIT = 56 * 1024 * 1024


def _layer_norm(x, g, b, eps=1e-5):
    mu = jnp.mean(x, axis=-1, keepdims=True)
    xc = x - mu
    var = jnp.mean(xc * xc, axis=-1, keepdims=True)
    return xc * lax.rsqrt(var + eps) * g + b


def _mm(a, b):
    return jnp.dot(a.astype(BF16), b.astype(BF16), preferred_element_type=F32)


def _mm_nt(a, b):
    return lax.dot_general(a.astype(BF16), b.astype(BF16), (((1,), (1,)), ((), ())),
                           preferred_element_type=F32)


def _mm_tn(a, b):
    return lax.dot_general(a.astype(BF16), b.astype(BF16), (((0,), (0,)), ((), ())),
                           preferred_element_type=F32)


def _split(x):
    hi = x.astype(BF16)
    return hi, (x - hi.astype(F32)).astype(BF16)


def _mm2_split(a, b):
    dot = functools.partial(jnp.dot, preferred_element_type=F32)
    return dot(a[0], b[0]) + dot(a[0], b[1])


def _mm_01(m01, x):
    hi = x.astype(BF16)
    r = x - hi.astype(F32)
    mid = r.astype(BF16)
    lo = (r - mid.astype(F32)).astype(BF16)
    dot = functools.partial(jnp.dot, preferred_element_type=F32)
    return dot(m01, hi) + dot(m01, mid) + dot(m01, lo)


def _softplus(x):
    return jnp.maximum(x, 0.0) + jnp.log1p(jnp.exp(-jnp.abs(x)))


PROJ_SPLITS = (DN_QKV, DN_QK, D_SSM, SSM_XBC, GATE_LANES)
PROJ_COLS = sum(PROJ_SPLITS)


def _ln_proj_kernel(x_ref, g_ref, b_ref, w_ref, qkv_ref, z_ref, sz_ref, xbc_ref, gate_ref):
    xn = _layer_norm(x_ref[...], g_ref[...], b_ref[...]).astype(BF16)
    start = 0
    for out_ref, width in zip((qkv_ref, z_ref, sz_ref, xbc_ref, gate_ref), PROJ_SPLITS):
        out_ref[...] = jnp.dot(xn, w_ref[:, start:start + width], preferred_element_type=F32)
        start += width


def _ln_proj(x, g, b, w, tm):
    t = x.shape[0]
    row = lambda i: (i, 0)
    fixed = lambda i: (0, 0)
    return pl.pallas_call(
        _ln_proj_kernel,
        grid=(t // tm,),
        in_specs=[pl.BlockSpec((tm, D_MODEL), row),
                  pl.BlockSpec((1, D_MODEL), fixed),
                  pl.BlockSpec((1, D_MODEL), fixed),
                  pl.BlockSpec((D_MODEL, PROJ_COLS), fixed, pipeline_mode=pl.Buffered(1))],
        out_specs=[pl.BlockSpec((tm, width), row) for width in PROJ_SPLITS],
        out_shape=[jax.ShapeDtypeStruct((t, width), F32) for width in PROJ_SPLITS],
        compiler_params=pltpu.CompilerParams(dimension_semantics=("parallel",),
                                             vmem_limit_bytes=VMEM_LIMIT),
        name="ln_proj",
    )(x, g, b, w)


CONV_PAD = 8
SOLVE_SPLIT_LEVELS = 3


def _causal_conv(buf_ref, x, w, chunk):
    buf_ref[CONV_PAD:CONV_PAD + chunk, :] = x
    first = CONV_PAD - (CONV_K - 1)
    y = w[0:1] * buf_ref[first:first + chunk, :]
    for i in range(1, CONV_K):
        y = y + w[i:i + 1] * buf_ref[first + i:first + i + chunk, :]
    tail = buf_ref[first + chunk:CONV_PAD + chunk, :]
    buf_ref[first:CONV_PAD, :] = tail
    return y, tail


def _stack_lanes(x, first, count, width):
    return jnp.concatenate(
        [x[:, (first + i) * width:(first + i + 1) * width] for i in range(count)], axis=0)


def _unstack_lanes(xs, count, chunk):
    return jnp.concatenate([xs[i * chunk:(i + 1) * chunk, :] for i in range(count)], axis=1)


def _stack_rows(xt, first, count):
    return jnp.concatenate([xt[first + i:first + i + 1, :] for i in range(count)], axis=1)


def _repeat_rows(v, first, count, reps):
    return jnp.concatenate(
        [jnp.broadcast_to(v[:, first + i:first + i + 1], (reps, 1)) for i in range(count)], axis=0)


def _spread_heads(a, first):
    return jnp.concatenate(
        [jnp.broadcast_to(a[:, first + r:first + r + 1], (a.shape[0], SSM_HEAD))
         for r in range(HEADS_PER_GROUP)], axis=1)


def _block_diag(x, count, own_block):
    return jnp.where(own_block, jnp.concatenate([x] * count, axis=1), 0.0)


def _transpose_rows(x, chunk):
    if chunk < LANES:
        x = jnp.concatenate([x, jnp.zeros((LANES - chunk, LANES), F32)], axis=0)
    return x.T[:, :chunk]


def _mixer_kernel(qkv_ref, z_ref, sz_ref, xbc_ref, gate_ref,
                  dnc0_ref, dnr0_ref, ssc0_ref, ssr0_ref,
                  dncw_ref, dna_ref, dndt_ref, dng_ref,
                  sscw_ref, sscb_ref, ssa_ref, ssdt_ref, ssd_ref, ssg_ref,
                  om_ref, dnc_ref, dnr_ref, ssc_ref, ssr_ref,
                  dn_buf, ss_buf, ht_ref, *, chunk, n_chunks):
    c = pl.program_id(1)

    @pl.when(c == 0)
    def _init():
        dn_buf[CONV_PAD - (CONV_K - 1):CONV_PAD, :] = dnc0_ref[0]
        ss_buf[CONV_PAD - (CONV_K - 1):CONV_PAD, :] = ssc0_ref[0]
        dnr_ref[0] = dnr0_ref[0]
        for g in range(SSM_GROUPS):
            rows = jnp.concatenate(
                [ssr0_ref[0, g * HEADS_PER_GROUP + r] for r in range(HEADS_PER_GROUP)], axis=0)
            ht_ref[g] = rows.T

    stack = min(DN_HEADS, MXU_DIM // chunk)
    ss_stack = min(stack, HEADS_PER_GROUP)
    shift = int(math.log2(chunk))
    n_squarings = shift - 1

    def block_masks(n):
        r_i = lax.broadcasted_iota(jnp.int32, (n, n), 0)
        c_i = lax.broadcasted_iota(jnp.int32, (n, n), 1)
        same = (r_i >> shift) == (c_i >> shift)
        return same & (r_i >= c_i), same & (r_i > c_i)

    causal, strict = block_masks(stack * chunk)
    causal_ss = causal if ss_stack == stack else block_masks(ss_stack * chunk)[0]
    head_of_row = lax.broadcasted_iota(jnp.int32, (stack * chunk, stack * DN_HEAD), 0) >> shift
    head_of_lane = (lax.broadcasted_iota(jnp.int32, (stack * chunk, stack * DN_HEAD), 1)
                    >> int(math.log2(DN_HEAD)))
    own_head = head_of_row == head_of_lane
    tril01 = (lax.broadcasted_iota(jnp.int32, (chunk, chunk), 0)
              >= lax.broadcasted_iota(jnp.int32, (chunk, chunk), 1)).astype(BF16)

    gate = gate_ref[...]
    beta = jax.nn.sigmoid(gate[:, 0:DN_HEADS])
    g_dn = -jnp.exp(dna_ref[...]) * _softplus(gate[:, DN_HEADS:2 * DN_HEADS] + dndt_ref[...])
    dt = _softplus(gate[:, 2 * DN_HEADS:2 * DN_HEADS + SSM_HEADS] + ssdt_ref[...])
    da = dt * (-jnp.exp(ssa_ref[...]))
    logdec = jnp.concatenate(
        [g_dn, da, jnp.zeros((chunk, LANES - DN_HEADS - SSM_HEADS), F32)], axis=1)
    cum = _mm_01(tril01, logdec)
    cum_t = _transpose_rows(cum, chunk)
    cum_last = cum[chunk - 1:chunk, :]

    conv, dn_tail = _causal_conv(dn_buf, qkv_ref[...], dncw_ref[...], chunk)
    qkv = jax.nn.silu(conv)
    dn_groups = []
    for g in range(DN_HEADS // stack):
        h0 = g * stack
        qs = _stack_lanes(qkv, h0, stack, DN_HEAD)
        ks = _stack_lanes(qkv, DN_HEADS + h0, stack, DN_HEAD)
        vs = _stack_lanes(qkv, 2 * DN_HEADS + h0, stack, DN_HEAD)
        qn = qs * lax.rsqrt(jnp.sum(qs * qs, axis=-1, keepdims=True) + 1e-6) * (DN_HEAD ** -0.5)
        kn = ks * lax.rsqrt(jnp.sum(ks * ks, axis=-1, keepdims=True) + 1e-6)
        cc = _stack_lanes(cum, h0, stack, 1)
        bcol = _stack_lanes(beta, h0, stack, 1)
        decay = jnp.exp(jnp.where(causal, cc - _stack_rows(cum_t, h0, stack), -jnp.inf))
        ecc = jnp.exp(cc)
        a_mat = jnp.where(strict, bcol * _mm_nt(kn, kn) * decay, 0.0)
        rhs = jnp.concatenate([bcol * vs, (bcol * ecc) * kn], axis=1)
        qk = jnp.where(causal, _mm_nt(qn, kn) * decay, 0.0)
        kd = jnp.exp(_repeat_rows(cum_last, h0, stack, chunk) - cc) * kn
        dn_groups.append((qn, ecc, a_mat, rhs, qk, kd))

    pows = [_split(grp[2]) for grp in dn_groups]
    sols = [grp[3] - _mm2_split(p, _split(grp[3])) for p, grp in zip(pows, dn_groups)]
    for level in range(n_squarings):
        if level < SOLVE_SPLIT_LEVELS:
            pows = [_split(_mm2_split(p, p)) for p in pows]
            sols = [s + _mm2_split(p, _split(s)) for p, s in zip(pows, sols)]
        else:
            pows = [(jnp.dot(p[0], p[0], preferred_element_type=F32).astype(BF16),) for p in pows]
            sols = [s + jnp.dot(p[0], s.astype(BF16), preferred_element_type=F32)
                    for p, s in zip(pows, sols)]

    conv, ss_tail = _causal_conv(ss_buf, xbc_ref[...], sscw_ref[...], chunk)
    xbc = jax.nn.silu(conv + sscb_ref[...])
    cum_ss = cum[:, DN_HEADS:DN_HEADS + SSM_HEADS]
    end_ss = cum_last[:, DN_HEADS:DN_HEADS + SSM_HEADS]
    ecum_ss = jnp.exp(cum_ss)
    tok_w = dt * jnp.exp(end_ss - cum_ss)
    e_end_ss = jnp.exp(end_ss)
    d_skip = ssd_ref[...]
    y_groups = []
    for g in range(SSM_GROUPS):
        g0 = g * HEADS_PER_GROUP
        b_g = xbc[:, D_SSM + g * SSM_STATE:D_SSM + (g + 1) * SSM_STATE]
        c_g = xbc[:, D_SSM + SSM_BC + g * SSM_STATE:D_SSM + SSM_BC + (g + 1) * SSM_STATE]
        cb = _mm_nt(c_g, b_g)
        cb_wide = jnp.concatenate([cb] * ss_stack, axis=1)
        cb_tiled = jnp.concatenate([cb_wide] * ss_stack, axis=0)
        intra = []
        for h0 in range(g0, g0 + HEADS_PER_GROUP, ss_stack):
            cc = _stack_lanes(cum_ss, h0, ss_stack, 1)
            lmat = jnp.exp(jnp.where(causal_ss, cc - _stack_rows(cum_t, DN_HEADS + h0, ss_stack),
                                     -jnp.inf))
            xdt = _stack_lanes(xbc, h0, ss_stack, SSM_HEAD) * _stack_lanes(dt, h0, ss_stack, 1)
            intra.append(_unstack_lanes(_mm(lmat * cb_tiled, xdt), ss_stack, chunk))
        x_g = xbc[:, g0 * SSM_HEAD:(g0 + HEADS_PER_GROUP) * SSM_HEAD]
        ht = ht_ref[g]
        y_g = (jnp.concatenate(intra, axis=1)
               + _spread_heads(ecum_ss, g0) * _mm(c_g, ht)
               + _spread_heads(d_skip, g0) * x_g)
        y_groups.append(y_g)
        xw = x_g * _spread_heads(tok_w, g0)
        ht_ref[g] = _spread_heads(e_end_ss, g0) * ht + _mm_tn(b_g, xw)
    y = jnp.concatenate(y_groups, axis=1) * jax.nn.silu(sz_ref[...])
    y = y * lax.rsqrt(jnp.mean(y * y, axis=-1, keepdims=True) + 1e-6) * ssg_ref[...]
    om_ref[:, DN_QK:D_MIX] = y

    dn_gain = dng_ref[...]
    e_end_dn = jnp.exp(cum_last)
    o_groups = []
    n_rows = stack * chunk
    for g, (qn, ecc, _, _, qk, kd) in enumerate(dn_groups):
        h0 = g * stack
        s = jnp.concatenate([dnr_ref[0, h0 + i] for i in range(stack)], axis=0)
        u = sols[g][:, :DN_HEAD]
        w = sols[g][:, DN_HEAD:]
        wq = _mm(jnp.concatenate([_block_diag(w, stack, own_head),
                                  _block_diag(qn, stack, own_head)], axis=0), s)
        delta = u - wq[:n_rows]
        o = ecc * wq[n_rows:] + _mm(qk, delta)
        s_new = (_repeat_rows(e_end_dn, h0, stack, DN_HEAD) * s
                 + _mm_tn(_block_diag(kd, stack, own_head), delta))
        for i in range(stack):
            dnr_ref[0, h0 + i] = s_new[i * DN_HEAD:(i + 1) * DN_HEAD, :]
        on = o * lax.rsqrt(jnp.mean(o * o, axis=-1, keepdims=True) + 1e-6) * dn_gain
        o_groups.append(_unstack_lanes(on, stack, chunk))
    om_ref[:, 0:DN_QK] = jnp.concatenate(o_groups, axis=1) * jax.nn.silu(z_ref[...])

    @pl.when(c == n_chunks - 1)
    def _finish():
        dnc_ref[0] = dn_tail
        ssc_ref[0] = ss_tail
        for g in range(SSM_GROUPS):
            rows = ht_ref[g].T
            for r in range(HEADS_PER_GROUP):
                ssr_ref[0, g * HEADS_PER_GROUP + r] = rows[r * SSM_HEAD:(r + 1) * SSM_HEAD, :]


def _mixer(proj, init, weights, n_seq, length, chunk, shared_init):
    n_chunks = length // chunk
    tok = lambda b, c: (b * n_chunks + c, 0)
    if shared_init:
        st3 = lambda b, c: (0, 0, 0)
        st4 = lambda b, c: (0, 0, 0, 0)
    else:
        st3 = lambda b, c: (b, 0, 0)
        st4 = lambda b, c: (b, 0, 0, 0)
    out3 = lambda b, c: (b, 0, 0)
    out4 = lambda b, c: (b, 0, 0, 0)
    fixed = lambda b, c: (0, 0)
    state_shapes = ((CONV_K - 1, DN_QKV), (DN_HEADS, DN_HEAD, DN_HEAD),
                    (CONV_K - 1, SSM_XBC), (SSM_HEADS, SSM_HEAD, SSM_STATE))

    def state_spec(shape, m3, m4):
        return pl.BlockSpec((1,) + shape, m3 if len(shape) == 2 else m4)

    in_specs = [pl.BlockSpec((chunk, width), tok) for width in PROJ_SPLITS]
    in_specs += [state_spec(s, st3, st4) for s in state_shapes]
    in_specs += [pl.BlockSpec(w.shape, fixed) for w in weights]
    out_specs = [pl.BlockSpec((chunk, D_MIX), tok)]
    out_specs += [state_spec(s, out3, out4) for s in state_shapes]
    out_shape = [jax.ShapeDtypeStruct((n_seq * length, D_MIX), F32)]
    out_shape += [jax.ShapeDtypeStruct((n_seq,) + s, F32) for s in state_shapes]
    outs = pl.pallas_call(
        functools.partial(_mixer_kernel, chunk=chunk, n_chunks=n_chunks),
        grid=(n_seq, n_chunks),
        in_specs=in_specs,
        out_specs=out_specs,
        out_shape=out_shape,
        scratch_shapes=[pltpu.VMEM((CONV_PAD + chunk, DN_QKV), F32),
                        pltpu.VMEM((CONV_PAD + chunk, SSM_XBC), F32),
                        pltpu.VMEM((SSM_GROUPS, SSM_STATE, HEADS_PER_GROUP * SSM_HEAD), F32)],
        compiler_params=pltpu.CompilerParams(dimension_semantics=("parallel", "arbitrary"),
                                             vmem_limit_bytes=VMEM_LIMIT),
        name=f"mixer_c{chunk}",
    )(*proj, *init, *weights)
    return outs[0], tuple(outs[1:])


def _out_ln1_kernel(x_ref, om_ref, w_ref, g0_ref, b0_ref, g1_ref, b1_ref, h_ref):
    h0 = _layer_norm(x_ref[...], g0_ref[...], b0_ref[...])
    mix = jnp.dot(om_ref[...].astype(BF16), w_ref[...], preferred_element_type=F32)
    h_ref[...] = _layer_norm(ALPHA * h0 + mix, g1_ref[...], b1_ref[...])


def _out_ln1(x, om, w, g0, b0, g1, b1, tm):
    t = x.shape[0]
    row = lambda i: (i, 0)
    fixed = lambda i: (0, 0)
    vec = pl.BlockSpec((1, D_MODEL), fixed)
    return pl.pallas_call(
        _out_ln1_kernel,
        grid=(t // tm,),
        in_specs=[pl.BlockSpec((tm, D_MODEL), row), pl.BlockSpec((tm, D_MIX), row),
                  pl.BlockSpec((D_MIX, D_MODEL), fixed), vec, vec, vec, vec],
        out_specs=pl.BlockSpec((tm, D_MODEL), row),
        out_shape=jax.ShapeDtypeStruct((t, D_MODEL), F32),
        compiler_params=pltpu.CompilerParams(dimension_semantics=("parallel",),
                                             vmem_limit_bytes=VMEM_LIMIT),
        name="out_ln1",
    )(x, om, w, g0, b0, g1, b1)


PEER_TOKENS = 512
PEER_SUBKEYS = 4
PEER_BLOCK = PEER_SUBKEYS * N_KEYS
PEER_BLOCKS = N_KEYS // PEER_SUBKEYS
PEER_STEPS = PEER_BLOCKS // 2
PAIR_LIMITS = tuple(TOP_K // (j + 1) for j in range(TOP_K))


def _top_distinct(s):
    vals, counts = [], []
    for _ in range(TOP_K):
        m = jnp.max(s, axis=0, keepdims=True)
        eq = s == m
        counts.append(jnp.sum(jnp.where(eq, 1.0, 0.0), axis=0, keepdims=True))
        vals.append(m)
        s = jnp.where(eq, -jnp.inf, s)
    return vals, counts


def _peer_route(q1, q2, k1, k2):
    s1 = jnp.dot(k1, q1, preferred_element_type=F32)
    s2 = jnp.dot(k2, q2, preferred_element_type=F32)
    a, na = _top_distinct(s1)
    b, nb = _top_distinct(s2)
    cand, mult, gate = [], [], []
    for j in range(TOP_K):
        ea = jnp.exp(a[j] - a[0])
        for l in range(PAIR_LIMITS[j]):
            cand.append(a[j] + b[l])
            mult.append(na[j] * nb[l])
            gate.append(ea * jnp.exp(b[l] - b[0]))
    cand = jnp.concatenate(cand, axis=0)
    mult = jnp.concatenate(mult, axis=0)
    gate = jnp.concatenate(gate, axis=0)
    above = jnp.zeros_like(cand)
    for i in range(cand.shape[0]):
        above = above + jnp.where(cand[i:i + 1, :] > cand, mult[i:i + 1, :], 0.0)
    theta = jnp.min(jnp.where(above < TOP_K, cand, jnp.inf), axis=0, keepdims=True)
    zsum = jnp.sum(jnp.where(cand >= theta, mult * gate, 0.0), axis=0, keepdims=True)
    n1 = jnp.where(s1 + b[TOP_K - 1] >= theta, 1.0, 0.0)
    passed = []
    for step in (8, 4, 2, 1):
        probes = [b[i + step - 1] for i in range(0, TOP_K, 2 * step)]
        for ok in reversed(passed):
            probes = [jnp.where(ok, probes[i + 1], probes[i]) for i in range(0, len(probes), 2)]
        ok = s1 + probes[0] >= theta
        passed.append(ok)
        n1 = n1 + jnp.where(ok, float(step), 0.0)
    rank2 = jnp.zeros_like(s2)
    for l in range(TOP_K):
        rank2 = rank2 + jnp.where(b[l] > s2, 1.0, 0.0)
    w1 = jnp.exp(s1 - a[0]) / zsum
    e2 = jnp.exp(s2 - b[0])
    return n1, w1, rank2.astype(BF16), e2.astype(BF16)


def _packed_bf16(words_ref):
    return pltpu.bitcast(words_ref[...], BF16)


def _peer_gated_act(p_ref, hid_ref, first_key, n1_ref, w1_ref, rank2_ref, e2_ref):
    sqrt_half = np.float32(np.sqrt(0.5))
    tile = (N_KEYS, LANES)
    for k in range(PEER_SUBKEYS):
        i1 = first_key + k
        rows = slice(k * N_KEYS, (k + 1) * N_KEYS)
        n_rows = [n1_ref[h, pl.ds(i1, 1), :] for h in range(PEER_HEADS)]
        w_rows = [w1_ref[h, pl.ds(i1, 1), :] for h in range(PEER_HEADS)]
        for s in range(PEER_TOKENS // LANES):
            lanes = slice(s * LANES, (s + 1) * LANES)
            gate = jnp.zeros(tile, BF16)
            for h in range(PEER_HEADS):
                n_row = jnp.broadcast_to(n_rows[h][:, lanes], tile).astype(BF16)
                w_row = jnp.broadcast_to(w_rows[h][:, lanes], tile).astype(BF16)
                e2 = e2_ref[h, :, lanes]
                sel = jnp.where(rank2_ref[h, :, lanes] < n_row, e2, jnp.zeros_like(e2))
                gate = gate + sel * w_row
            hk = hid_ref[rows, lanes]
            act = 0.5 * hk * (1.0 + lax.erf(hk * sqrt_half))
            words = slice(k * (N_KEYS // 2), (k + 1) * (N_KEYS // 2))
            p_ref[words, lanes] = pltpu.bitcast(gate * act.astype(BF16), jnp.uint32)


def _peer_kernel(h_ref, wq_ref, keys_ref, u0_ref, ua_ref, ub_ref, va_ref, vb_ref, g2_ref, b2_ref,
                 y_ref, xt_ref, qt_ref, n1_ref, w1_ref, rank2_ref, e2_ref,
                 hid_a, hid_b, p_a, p_b, acc_ref):
    j = pl.program_id(1)
    dot = functools.partial(jnp.dot, preferred_element_type=F32)

    @pl.when(j == 0)
    def _route():
        xt = h_ref[...].T.astype(BF16)
        xt_ref[...] = xt
        qt_ref[...] = dot(wq_ref[...], xt)

        def route_head(h, carry):
            base = pl.multiple_of(h * (2 * D_HALF), 2 * D_HALF)
            n1, w1, rank2, e2 = _peer_route(
                qt_ref[pl.ds(base, D_HALF), :].astype(BF16),
                qt_ref[pl.ds(base + D_HALF, D_HALF), :].astype(BF16),
                keys_ref[2 * h], keys_ref[2 * h + 1])
            n1_ref[h] = n1
            w1_ref[h] = w1
            rank2_ref[h] = rank2
            e2_ref[h] = e2
            return carry

        lax.fori_loop(0, PEER_HEADS, route_head, 0)
        hid_a[...] = dot(u0_ref[...], xt)
        p_b[...] = jnp.zeros_like(p_b)
        acc_ref[...] = jnp.zeros_like(acc_ref)

    @pl.when(j < PEER_STEPS)
    def _step():
        xt = xt_ref[...]
        tables = (n1_ref, w1_ref, rank2_ref, e2_ref)
        _peer_gated_act(p_a, hid_a, 2 * j * PEER_SUBKEYS, *tables)
        hid_b[...] = dot(ua_ref[...], xt)
        acc_ref[...] += dot(va_ref[...], _packed_bf16(p_b))
        _peer_gated_act(p_b, hid_b, (2 * j + 1) * PEER_SUBKEYS, *tables)
        hid_a[...] = dot(ub_ref[...], xt)
        acc_ref[...] += dot(vb_ref[...], _packed_bf16(p_a))

    @pl.when(j == PEER_STEPS)
    def _finish():
        ff = acc_ref[...] + dot(va_ref[...], _packed_bf16(p_b))
        y_ref[...] = _layer_norm(ALPHA * h_ref[...] + ff.T, g2_ref[...], b2_ref[...])


def _peer(h, wq_t, keys, u, v_t, g2, b2):
    t = h.shape[0]
    last = PEER_BLOCKS - 1
    tok = lambda i, j: (i, 0)
    fixed2 = lambda i, j: (0, 0)
    u_spec = lambda index: pl.BlockSpec((PEER_BLOCK, D_MODEL), index)
    v_spec = lambda index: pl.BlockSpec((D_MODEL, PEER_BLOCK), index)
    head_tile = (PEER_HEADS, N_KEYS, PEER_TOKENS)
    return pl.pallas_call(
        _peer_kernel,
        grid=(t // PEER_TOKENS, PEER_STEPS + 1),
        in_specs=[pl.BlockSpec((PEER_TOKENS, D_MODEL), tok),
                  pl.BlockSpec(wq_t.shape, fixed2),
                  pl.BlockSpec(keys.shape, lambda i, j: (0, 0, 0)),
                  u_spec(fixed2),
                  u_spec(lambda i, j: (jnp.minimum(2 * j + 1, last), 0)),
                  u_spec(lambda i, j: (jnp.minimum(2 * j + 2, last), 0)),
                  v_spec(lambda i, j: (0, jnp.maximum(2 * j - 1, 0))),
                  v_spec(lambda i, j: (0, jnp.minimum(2 * j, last))),
                  pl.BlockSpec((1, D_MODEL), fixed2),
                  pl.BlockSpec((1, D_MODEL), fixed2)],
        out_specs=pl.BlockSpec((PEER_TOKENS, D_MODEL), tok),
        out_shape=jax.ShapeDtypeStruct((t, D_MODEL), F32),
        scratch_shapes=[pltpu.VMEM((D_MODEL, PEER_TOKENS), BF16),
                        pltpu.VMEM((PEER_HEADS * 2 * D_HALF, PEER_TOKENS), F32),
                        pltpu.VMEM(head_tile, F32), pltpu.VMEM(head_tile, F32),
                        pltpu.VMEM(head_tile, BF16), pltpu.VMEM(head_tile, BF16),
                        pltpu.VMEM((PEER_BLOCK, PEER_TOKENS), F32),
                        pltpu.VMEM((PEER_BLOCK, PEER_TOKENS), F32),
                        pltpu.VMEM((PEER_BLOCK // 2, PEER_TOKENS), jnp.uint32),
                        pltpu.VMEM((PEER_BLOCK // 2, PEER_TOKENS), jnp.uint32),
                        pltpu.VMEM((D_MODEL, PEER_TOKENS), F32)],
        compiler_params=pltpu.CompilerParams(dimension_semantics=("parallel", "arbitrary"),
                                             vmem_limit_bytes=VMEM_LIMIT),
        name="peer",
    )(h, wq_t, keys, u, u, u, v_t, v_t, g2, b2)


def kernel(x_prompt, x_sample, state_dn_conv, state_dn_rec, state_ssm_conv, state_ssm_rec, meta_tokens, ln_in_g, ln_in_b, w_in, dn_conv_w, dn_A_log, dn_dt_bias, dn_norm_g, ssm_conv_w, ssm_conv_b, ssm_A_log, ssm_dt_bias, ssm_D, ssm_norm_g, w_out, ln1_g, ln1_b, peer_w_q, peer_keys, peer_u, peer_v, ln2_g, ln2_b):
    n_prompt, seq, _ = x_prompt.shape
    n_sample, dec_seq, _ = x_sample.shape
    row = lambda v: v.reshape(1, -1)

    w = w_in[0]
    o_z = DN_QKV
    o_b = o_z + DN_QK
    o_a = o_b + DN_HEADS
    o_sz = o_a + DN_HEADS
    o_xbc = o_sz + D_SSM
    o_dt = o_xbc + SSM_XBC
    gates = jnp.concatenate(
        [w[:, o_b:o_a], w[:, o_a:o_sz], w[:, o_dt:],
         jnp.zeros((D_MODEL, GATE_LANES - 2 * DN_HEADS - SSM_HEADS), w.dtype)], axis=1)
    w_proj = jnp.concatenate(
        [w[:, :o_z], w[:, o_z:o_b], w[:, o_sz:o_xbc], w[:, o_xbc:o_dt], gates],
        axis=1).astype(BF16)
    w_o = w_out[0].astype(BF16)
    wq_t = peer_w_q[0].T.astype(BF16)
    keys = peer_keys[0].reshape(PEER_HEADS * 2, N_KEYS, D_HALF).astype(BF16)
    u_tab = peer_u[0].astype(BF16)
    v_t = peer_v[0].T.astype(BF16)

    g_in, b_in = row(ln_in_g), row(ln_in_b)
    mixer_w = (dn_conv_w[0], row(dn_A_log[0]), row(dn_dt_bias[0]), row(dn_norm_g[0]),
               ssm_conv_w[0], row(ssm_conv_b[0]), row(ssm_A_log[0]), row(ssm_dt_bias[0]),
               row(ssm_D[0]), row(ssm_norm_g[0]))

    def trunk(x, init, n_seq, length, chunk, shared_init):
        proj = _ln_proj(x, g_in, b_in, w_proj, tm=min(256, x.shape[0]))
        om, states = _mixer(proj, init, mixer_w, n_seq, length, chunk, shared_init)
        return om, states

    zero_init = (jnp.zeros((1, CONV_K - 1, DN_QKV), F32),
                 jnp.zeros((1, DN_HEADS, DN_HEAD, DN_HEAD), F32),
                 jnp.zeros((1, CONV_K - 1, SSM_XBC), F32),
                 jnp.zeros((1, SSM_HEADS, SSM_HEAD, SSM_STATE), F32))
    _, meta_state = trunk(meta_tokens, zero_init, 1, N_META, N_META, False)

    xp = x_prompt.reshape(n_prompt * seq, D_MODEL)
    om_p, st_p = trunk(xp, meta_state, n_prompt, seq, CHUNK, True)
    xs = x_sample.reshape(n_sample * dec_seq, D_MODEL)
    init_s = (state_dn_conv[0], state_dn_rec[0], state_ssm_conv[0], state_ssm_rec[0])
    om_s, st_s = trunk(xs, init_s, n_sample, dec_seq, dec_seq, False)

    def tail(x, om):
        h1 = _out_ln1(x, om, w_o, g_in, b_in, row(ln1_g[0]), row(ln1_b[0]), tm=256)
        return _peer(h1, wq_t, keys, u_tab, v_t, row(ln2_g[0]), row(ln2_b[0]))

    y_p = tail(xp, om_p).reshape(n_prompt, seq, D_MODEL)
    y_s = tail(xs, om_s).reshape(n_sample, dec_seq, D_MODEL)
    return (y_p, y_s) + tuple(s[None] for s in st_p) + tuple(s[None] for s in st_s)
```

```python
import functools
import math

import numpy as np
import jax
import jax.numpy as jnp
from jax import lax
from jax.experimental import pallas as pl
from jax.experimental.pallas import tpu as pltpu

F32 = jnp.float32
BF16 = jnp.bfloat16

D_MODEL = 1024
N_META = 16
DN_HEADS = 16
DN_HEAD = 64
DN_QK = DN_HEADS * DN_HEAD
DN_QKV = 3 * DN_QK
SSM_HEADS = 16
SSM_HEAD = 64
SSM_GROUPS = 2
SSM_STATE = 128
D_SSM = SSM_HEADS * SSM_HEAD
SSM_BC = SSM_GROUPS * SSM_STATE
SSM_XBC = D_SSM + 2 * SSM_BC
HEADS_PER_GROUP = SSM_HEADS // SSM_GROUPS
D_MIX = DN_QK + D_SSM
CONV_K = 4
CHUNK = 64
PEER_HEADS = 8
N_KEYS = 128
D_HALF = 128
TOP_K = 16
ALPHA = 2.0 ** 0.25
GATE_LANES = 128
LANES = 128
MXU_DIM = 256
VMEM_LIMIT = 56 * 1024 * 1024


def _layer_norm(x, g, b, eps=1e-5):
    mu = jnp.mean(x, axis=-1, keepdims=True)
    xc = x - mu
    var = jnp.mean(xc * xc, axis=-1, keepdims=True)
    return xc * lax.rsqrt(var + eps) * g + b


def _mm(a, b):
    return jnp.dot(a.astype(BF16), b.astype(BF16), preferred_element_type=F32)


def _mm_nt(a, b):
    return lax.dot_general(a.astype(BF16), b.astype(BF16), (((1,), (1,)), ((), ())),
                           preferred_element_type=F32)


def _mm_tn(a, b):
    return lax.dot_general(a.astype(BF16), b.astype(BF16), (((0,), (0,)), ((), ())),
                           preferred_element_type=F32)


def _split(x):
    hi = x.astype(BF16)
    return hi, (x - hi.astype(F32)).astype(BF16)


def _mm2_split(a, b):
    dot = functools.partial(jnp.dot, preferred_element_type=F32)
    return dot(a[0], b[0]) + dot(a[0], b[1])


def _mm_01(m01, x):
    hi = x.astype(BF16)
    r = x - hi.astype(F32)
    mid = r.astype(BF16)
    lo = (r - mid.astype(F32)).astype(BF16)
    dot = functools.partial(jnp.dot, preferred_element_type=F32)
    return dot(m01, hi) + dot(m01, mid) + dot(m01, lo)


def _softplus(x):
    return jnp.maximum(x, 0.0) + jnp.log1p(jnp.exp(-jnp.abs(x)))


PROJ_SPLITS = (DN_QKV, DN_QK, D_SSM, SSM_XBC, GATE_LANES)
PROJ_COLS = sum(PROJ_SPLITS)


def _ln_proj_kernel(x_ref, g_ref, b_ref, w_ref, qkv_ref, z_ref, sz_ref, xbc_ref, gate_ref):
    xn = _layer_norm(x_ref[...], g_ref[...], b_ref[...]).astype(BF16)
    start = 0
    for out_ref, width in zip((qkv_ref, z_ref, sz_ref, xbc_ref, gate_ref), PROJ_SPLITS):
        out_ref[...] = jnp.dot(xn, w_ref[:, start:start + width], preferred_element_type=F32)
        start += width


def _ln_proj(x, g, b, w, tm):
    t = x.shape[0]
    row = lambda i: (i, 0)
    fixed = lambda i: (0, 0)
    return pl.pallas_call(
        _ln_proj_kernel,
        grid=(t // tm,),
        in_specs=[pl.BlockSpec((tm, D_MODEL), row),
                  pl.BlockSpec((1, D_MODEL), fixed),
                  pl.BlockSpec((1, D_MODEL), fixed),
                  pl.BlockSpec((D_MODEL, PROJ_COLS), fixed, pipeline_mode=pl.Buffered(1))],
        out_specs=[pl.BlockSpec((tm, width), row) for width in PROJ_SPLITS],
        out_shape=[jax.ShapeDtypeStruct((t, width), F32) for width in PROJ_SPLITS],
        compiler_params=pltpu.CompilerParams(dimension_semantics=("parallel",),
                                             vmem_limit_bytes=VMEM_LIMIT),
        name="ln_proj",
    )(x, g, b, w)


CONV_PAD = 8
SOLVE_SPLIT_LEVELS = 3


def _causal_conv(buf_ref, x, w, chunk):
    buf_ref[CONV_PAD:CONV_PAD + chunk, :] = x
    first = CONV_PAD - (CONV_K - 1)
    y = w[0:1] * buf_ref[first:first + chunk, :]
    for i in range(1, CONV_K):
        y = y + w[i:i + 1] * buf_ref[first + i:first + i + chunk, :]
    tail = buf_ref[first + chunk:CONV_PAD + chunk, :]
    buf_ref[first:CONV_PAD, :] = tail
    return y, tail


def _stack_lanes(x, first, count, width):
    return jnp.concatenate(
        [x[:, (first + i) * width:(first + i + 1) * width] for i in range(count)], axis=0)


def _unstack_lanes(xs, count, chunk):
    return jnp.concatenate([xs[i * chunk:(i + 1) * chunk, :] for i in range(count)], axis=1)


def _stack_rows(xt, first, count):
    return jnp.concatenate([xt[first + i:first + i + 1, :] for i in range(count)], axis=1)


def _repeat_rows(v, first, count, reps):
    return jnp.concatenate(
        [jnp.broadcast_to(v[:, first + i:first + i + 1], (reps, 1)) for i in range(count)], axis=0)


def _spread_heads(a, first):
    return jnp.concatenate(
        [jnp.broadcast_to(a[:, first + r:first + r + 1], (a.shape[0], SSM_HEAD))
         for r in range(HEADS_PER_GROUP)], axis=1)


def _block_diag(x, count, own_block):
    return jnp.where(own_block, jnp.concatenate([x] * count, axis=1), 0.0)


def _transpose_rows(x, chunk):
    if chunk < LANES:
        x = jnp.concatenate([x, jnp.zeros((LANES - chunk, LANES), F32)], axis=0)
    return x.T[:, :chunk]


def _mixer_kernel(qkv_ref, z_ref, sz_ref, xbc_ref, gate_ref,
                  dnc0_ref, dnr0_ref, ssc0_ref, ssr0_ref,
                  dncw_ref, dna_ref, dndt_ref, dng_ref,
                  sscw_ref, sscb_ref, ssa_ref, ssdt_ref, ssd_ref, ssg_ref,
                  om_ref, dnc_ref, dnr_ref, ssc_ref, ssr_ref,
                  dn_buf, ss_buf, ht_ref, *, chunk, n_chunks):
    c = pl.program_id(1)

    @pl.when(c == 0)
    def _init():
        dn_buf[CONV_PAD - (CONV_K - 1):CONV_PAD, :] = dnc0_ref[0]
        ss_buf[CONV_PAD - (CONV_K - 1):CONV_PAD, :] = ssc0_ref[0]
        dnr_ref[0] = dnr0_ref[0]
        for g in range(SSM_GROUPS):
            rows = jnp.concatenate(
                [ssr0_ref[0, g * HEADS_PER_GROUP + r] for r in range(HEADS_PER_GROUP)], axis=0)
            ht_ref[g] = rows.T

    stack = min(DN_HEADS, MXU_DIM // chunk)
    ss_stack = min(stack, HEADS_PER_GROUP)
    shift = int(math.log2(chunk))
    n_squarings = shift - 1

    def block_masks(n):
        r_i = lax.broadcasted_iota(jnp.int32, (n, n), 0)
        c_i = lax.broadcasted_iota(jnp.int32, (n, n), 1)
        same = (r_i >> shift) == (c_i >> shift)
        return same & (r_i >= c_i), same & (r_i > c_i)

    causal, strict = block_masks(stack * chunk)
    causal_ss = causal if ss_stack == stack else block_masks(ss_stack * chunk)[0]
    head_of_row = lax.broadcasted_iota(jnp.int32, (stack * chunk, stack * DN_HEAD), 0) >> shift
    head_of_lane = (lax.broadcasted_iota(jnp.int32, (stack * chunk, stack * DN_HEAD), 1)
                    >> int(math.log2(DN_HEAD)))
    own_head = head_of_row == head_of_lane
    tril01 = (lax.broadcasted_iota(jnp.int32, (chunk, chunk), 0)
              >= lax.broadcasted_iota(jnp.int32, (chunk, chunk), 1)).astype(BF16)

    gate = gate_ref[...]
    beta = jax.nn.sigmoid(gate[:, 0:DN_HEADS])
    g_dn = -jnp.exp(dna_ref[...]) * _softplus(gate[:, DN_HEADS:2 * DN_HEADS] + dndt_ref[...])
    dt = _softplus(gate[:, 2 * DN_HEADS:2 * DN_HEADS + SSM_HEADS] + ssdt_ref[...])
    da = dt * (-jnp.exp(ssa_ref[...]))
    logdec = jnp.concatenate(
        [g_dn, da, jnp.zeros((chunk, LANES - DN_HEADS - SSM_HEADS), F32)], axis=1)
    cum = _mm_01(tril01, logdec)
    cum_t = _transpose_rows(cum, chunk)
    cum_last = cum[chunk - 1:chunk, :]

    conv, dn_tail = _causal_conv(dn_buf, qkv_ref[...], dncw_ref[...], chunk)
    qkv = jax.nn.silu(conv)
    dn_groups = []
    for g in range(DN_HEADS // stack):
        h0 = g * stack
        qs = _stack_lanes(qkv, h0, stack, DN_HEAD)
        ks = _stack_lanes(qkv, DN_HEADS + h0, stack, DN_HEAD)
        vs = _stack_lanes(qkv, 2 * DN_HEADS + h0, stack, DN_HEAD)
        qn = qs * lax.rsqrt(jnp.sum(qs * qs, axis=-1, keepdims=True) + 1e-6) * (DN_HEAD ** -0.5)
        kn = ks * lax.rsqrt(jnp.sum(ks * ks, axis=-1, keepdims=True) + 1e-6)
        cc = _stack_lanes(cum, h0, stack, 1)
        bcol = _stack_lanes(beta, h0, stack, 1)
        decay = jnp.exp(jnp.where(causal, cc - _stack_rows(cum_t, h0, stack), -jnp.inf))
        ecc = jnp.exp(cc)
        a_mat = jnp.where(strict, bcol * _mm_nt(kn, kn) * decay, 0.0)
        rhs = jnp.concatenate([bcol * vs, (bcol * ecc) * kn], axis=1)
        qk = jnp.where(causal, _mm_nt(qn, kn) * decay, 0.0)
        kd = jnp.exp(_repeat_rows(cum_last, h0, stack, chunk) - cc) * kn
        dn_groups.append((qn, ecc, a_mat, rhs, qk, kd))

    pows = [_split(grp[2]) for grp in dn_groups]
    sols = [grp[3] - _mm2_split(p, _split(grp[3])) for p, grp in zip(pows, dn_groups)]
    for level in range(n_squarings):
        if level < SOLVE_SPLIT_LEVELS:
            pows = [_split(_mm2_split(p, p)) for p in pows]
            sols = [s + _mm2_split(p, _split(s)) for p, s in zip(pows, sols)]
        else:
            pows = [(jnp.dot(p[0], p[0], preferred_element_type=F32).astype(BF16),) for p in pows]
            sols = [s + jnp.dot(p[0], s.astype(BF16), preferred_element_type=F32)
                    for p, s in zip(pows, sols)]

    conv, ss_tail = _causal_conv(ss_buf, xbc_ref[...], sscw_ref[...], chunk)
    xbc = jax.nn.silu(conv + sscb_ref[...])
    cum_ss = cum[:, DN_HEADS:DN_HEADS + SSM_HEADS]
    end_ss = cum_last[:, DN_HEADS:DN_HEADS + SSM_HEADS]
    ecum_ss = jnp.exp(cum_ss)
    tok_w = dt * jnp.exp(end_ss - cum_ss)
    e_end_ss = jnp.exp(end_ss)
    d_skip = ssd_ref[...]
    y_groups = []
    for g in range(SSM_GROUPS):
        g0 = g * HEADS_PER_GROUP
        b_g = xbc[:, D_SSM + g * SSM_STATE:D_SSM + (g + 1) * SSM_STATE]
        c_g = xbc[:, D_SSM + SSM_BC + g * SSM_STATE:D_SSM + SSM_BC + (g + 1) * SSM_STATE]
        cb = _mm_nt(c_g, b_g)
        cb_wide = jnp.concatenate([cb] * ss_stack, axis=1)
        cb_tiled = jnp.concatenate([cb_wide] * ss_stack, axis=0)
        intra = []
        for h0 in range(g0, g0 + HEADS_PER_GROUP, ss_stack):
            cc = _stack_lanes(cum_ss, h0, ss_stack, 1)
            lmat = jnp.exp(jnp.where(causal_ss, cc - _stack_rows(cum_t, DN_HEADS + h0, ss_stack),
                                     -jnp.inf))
            xdt = _stack_lanes(xbc, h0, ss_stack, SSM_HEAD) * _stack_lanes(dt, h0, ss_stack, 1)
            intra.append(_unstack_lanes(_mm(lmat * cb_tiled, xdt), ss_stack, chunk))
        x_g = xbc[:, g0 * SSM_HEAD:(g0 + HEADS_PER_GROUP) * SSM_HEAD]
        ht = ht_ref[g]
        y_g = (jnp.concatenate(intra, axis=1)
               + _spread_heads(ecum_ss, g0) * _mm(c_g, ht)
               + _spread_heads(d_skip, g0) * x_g)
        y_groups.append(y_g)
        xw = x_g * _spread_heads(tok_w, g0)
        ht_ref[g] = _spread_heads(e_end_ss, g0) * ht + _mm_tn(b_g, xw)
    y = jnp.concatenate(y_groups, axis=1) * jax.nn.silu(sz_ref[...])
    y = y * lax.rsqrt(jnp.mean(y * y, axis=-1, keepdims=True) + 1e-6) * ssg_ref[...]
    om_ref[:, DN_QK:D_MIX] = y

    dn_gain = dng_ref[...]
    e_end_dn = jnp.exp(cum_last)
    o_groups = []
    n_rows = stack * chunk
    for g, (qn, ecc, _, _, qk, kd) in enumerate(dn_groups):
        h0 = g * stack
        s = jnp.concatenate([dnr_ref[0, h0 + i] for i in range(stack)], axis=0)
        u = sols[g][:, :DN_HEAD]
        w = sols[g][:, DN_HEAD:]
        wq = _mm(jnp.concatenate([_block_diag(w, stack, own_head),
                                  _block_diag(qn, stack, own_head)], axis=0), s)
        delta = u - wq[:n_rows]
        o = ecc * wq[n_rows:] + _mm(qk, delta)
        s_new = (_repeat_rows(e_end_dn, h0, stack, DN_HEAD) * s
                 + _mm_tn(_block_diag(kd, stack, own_head), delta))
        for i in range(stack):
            dnr_ref[0, h0 + i] = s_new[i * DN_HEAD:(i + 1) * DN_HEAD, :]
        on = o * lax.rsqrt(jnp.mean(o * o, axis=-1, keepdims=True) + 1e-6) * dn_gain
        o_groups.append(_unstack_lanes(on, stack, chunk))
    om_ref[:, 0:DN_QK] = jnp.concatenate(o_groups, axis=1) * jax.nn.silu(z_ref[...])

    @pl.when(c == n_chunks - 1)
    def _finish():
        dnc_ref[0] = dn_tail
        ssc_ref[0] = ss_tail
        for g in range(SSM_GROUPS):
            rows = ht_ref[g].T
            for r in range(HEADS_PER_GROUP):
                ssr_ref[0, g * HEADS_PER_GROUP + r] = rows[r * SSM_HEAD:(r + 1) * SSM_HEAD, :]


def _mixer(proj, init, weights, n_seq, length, chunk, shared_init):
    n_chunks = length // chunk
    tok = lambda b, c: (b * n_chunks + c, 0)
    if shared_init:
        st3 = lambda b, c: (0, 0, 0)
        st4 = lambda b, c: (0, 0, 0, 0)
    else:
        st3 = lambda b, c: (b, 0, 0)
        st4 = lambda b, c: (b, 0, 0, 0)
    out3 = lambda b, c: (b, 0, 0)
    out4 = lambda b, c: (b, 0, 0, 0)
    fixed = lambda b, c: (0, 0)
    state_shapes = ((CONV_K - 1, DN_QKV), (DN_HEADS, DN_HEAD, DN_HEAD),
                    (CONV_K - 1, SSM_XBC), (SSM_HEADS, SSM_HEAD, SSM_STATE))

    def state_spec(shape, m3, m4):
        return pl.BlockSpec((1,) + shape, m3 if len(shape) == 2 else m4)

    in_specs = [pl.BlockSpec((chunk, width), tok) for width in PROJ_SPLITS]
    in_specs += [state_spec(s, st3, st4) for s in state_shapes]
    in_specs += [pl.BlockSpec(w.shape, fixed) for w in weights]
    out_specs = [pl.BlockSpec((chunk, D_MIX), tok)]
    out_specs += [state_spec(s, out3, out4) for s in state_shapes]
    out_shape = [jax.ShapeDtypeStruct((n_seq * length, D_MIX), F32)]
    out_shape += [jax.ShapeDtypeStruct((n_seq,) + s, F32) for s in state_shapes]
    outs = pl.pallas_call(
        functools.partial(_mixer_kernel, chunk=chunk, n_chunks=n_chunks),
        grid=(n_seq, n_chunks),
        in_specs=in_specs,
        out_specs=out_specs,
        out_shape=out_shape,
        scratch_shapes=[pltpu.VMEM((CONV_PAD + chunk, DN_QKV), F32),
                        pltpu.VMEM((CONV_PAD + chunk, SSM_XBC), F32),
                        pltpu.VMEM((SSM_GROUPS, SSM_STATE, HEADS_PER_GROUP * SSM_HEAD), F32)],
        compiler_params=pltpu.CompilerParams(dimension_semantics=("parallel", "arbitrary"),
                                             vmem_limit_bytes=VMEM_LIMIT),
        name=f"mixer_c{chunk}",
    )(*proj, *init, *weights)
    return outs[0], tuple(outs[1:])


def _out_ln1_kernel(x_ref, om_ref, w_ref, g0_ref, b0_ref, g1_ref, b1_ref, h_ref):
    h0 = _layer_norm(x_ref[...], g0_ref[...], b0_ref[...])
    mix = jnp.dot(om_ref[...].astype(BF16), w_ref[...], preferred_element_type=F32)
    h_ref[...] = _layer_norm(ALPHA * h0 + mix, g1_ref[...], b1_ref[...])


def _out_ln1(x, om, w, g0, b0, g1, b1, tm):
    t = x.shape[0]
    row = lambda i: (i, 0)
    fixed = lambda i: (0, 0)
    vec = pl.BlockSpec((1, D_MODEL), fixed)
    return pl.pallas_call(
        _out_ln1_kernel,
        grid=(t // tm,),
        in_specs=[pl.BlockSpec((tm, D_MODEL), row), pl.BlockSpec((tm, D_MIX), row),
                  pl.BlockSpec((D_MIX, D_MODEL), fixed), vec, vec, vec, vec],
        out_specs=pl.BlockSpec((tm, D_MODEL), row),
        out_shape=jax.ShapeDtypeStruct((t, D_MODEL), F32),
        compiler_params=pltpu.CompilerParams(dimension_semantics=("parallel",),
                                             vmem_limit_bytes=VMEM_LIMIT),
        name="out_ln1",
    )(x, om, w, g0, b0, g1, b1)


PEER_TOKENS = 512
PEER_SUBKEYS = 4
PEER_BLOCK = PEER_SUBKEYS * N_KEYS
PEER_BLOCKS = N_KEYS // PEER_SUBKEYS
PEER_STEPS = PEER_BLOCKS // 2
PAIR_LIMITS = tuple(TOP_K // (j + 1) for j in range(TOP_K))


def _top_distinct(s, counted):
    vals, counts = [], []
    for _ in range(TOP_K):
        m = jnp.max(s, axis=0, keepdims=True)
        eq = s == m
        if counted:
            counts.append(jnp.sum(jnp.where(eq, 1.0, 0.0), axis=0, keepdims=True))
        vals.append(m)
        s = jnp.where(eq, -jnp.inf, s)
    used = jnp.sum(jnp.where(s == -jnp.inf, 1.0, 0.0), axis=0, keepdims=True)
    return vals, counts, used


def _peer_tables(s1, s2, a, na, b, nb):
    cand, mult, gate = [], [], []
    for j in range(TOP_K):
        ea = jnp.exp(a[j] - a[0])
        for l in range(PAIR_LIMITS[j]):
            cand.append(a[j] + b[l])
            mult.append(jnp.ones_like(a[0]) if na is None else na[j] * nb[l])
            gate.append(ea * jnp.exp(b[l] - b[0]))
    cand = jnp.concatenate(cand, axis=0)
    mult = jnp.concatenate(mult, axis=0)
    gate = jnp.concatenate(gate, axis=0)
    above = jnp.zeros_like(cand)
    for i in range(cand.shape[0]):
        above = above + jnp.where(cand[i:i + 1, :] > cand, mult[i:i + 1, :], 0.0)
    theta = jnp.min(jnp.where(above < TOP_K, cand, jnp.inf), axis=0, keepdims=True)
    zsum = jnp.sum(jnp.where(cand >= theta, mult * gate, 0.0), axis=0, keepdims=True)
    n1 = jnp.where(s1 + b[TOP_K - 1] >= theta, 1.0, 0.0)
    passed = []
    for step in (8, 4, 2, 1):
        probes = [b[i + step - 1] for i in range(0, TOP_K, 2 * step)]
        for ok in reversed(passed):
            probes = [jnp.where(ok, probes[i + 1], probes[i]) for i in range(0, len(probes), 2)]
        ok = s1 + probes[0] >= theta
        passed.append(ok)
        n1 = n1 + jnp.where(ok, float(step), 0.0)
    rank2 = jnp.zeros_like(s2)
    for l in range(TOP_K):
        rank2 = rank2 + jnp.where(b[l] > s2, 1.0, 0.0)
    w1 = jnp.exp(s1 - a[0]) / zsum
    e2 = jnp.exp(s2 - b[0])
    return n1, w1, rank2.astype(BF16), e2.astype(BF16)


def _packed_bf16(words_ref):
    return pltpu.bitcast(words_ref[...], BF16)


def _peer_gated_act(p_ref, hid_ref, first_key, n1_ref, w1_ref, rank2_ref, e2_ref):
    sqrt_half = np.float32(np.sqrt(0.5))
    tile = (N_KEYS, LANES)
    for k in range(PEER_SUBKEYS):
        i1 = first_key + k
        rows = slice(k * N_KEYS, (k + 1) * N_KEYS)
        n_rows = [n1_ref[h, pl.ds(i1, 1), :] for h in range(PEER_HEADS)]
        w_rows = [w1_ref[h, pl.ds(i1, 1), :] for h in range(PEER_HEADS)]
        for s in range(PEER_TOKENS // LANES):
            lanes = slice(s * LANES, (s + 1) * LANES)
            gate = jnp.zeros(tile, BF16)
            for h in range(PEER_HEADS):
                n_row = jnp.broadcast_to(n_rows[h][:, lanes], tile).astype(BF16)
                w_row = jnp.broadcast_to(w_rows[h][:, lanes], tile).astype(BF16)
                e2 = e2_ref[h, :, lanes]
                sel = jnp.where(rank2_ref[h, :, lanes] < n_row, e2, jnp.zeros_like(e2))
                gate = gate + sel * w_row
            hk = hid_ref[rows, lanes]
            act = 0.5 * hk * (1.0 + lax.erf(hk * sqrt_half))
            words = slice(k * (N_KEYS // 2), (k + 1) * (N_KEYS // 2))
            p_ref[words, lanes] = pltpu.bitcast(gate * act.astype(BF16), jnp.uint32)


def _peer_kernel(h_ref, wq_ref, keys_ref, u0_ref, ua_ref, ub_ref, va_ref, vb_ref, g2_ref, b2_ref,
                 y_ref, xt_ref, qt_ref, n1_ref, w1_ref, rank2_ref, e2_ref,
                 hid_a, hid_b, p_a, p_b, acc_ref):
    j = pl.program_id(1)
    dot = functools.partial(jnp.dot, preferred_element_type=F32)

    @pl.when(j == 0)
    def _route():
        xt = h_ref[...].T.astype(BF16)
        xt_ref[...] = xt
        qt_ref[...] = dot(wq_ref[...], xt)

        def route_head(h, carry):
            base = pl.multiple_of(h * (2 * D_HALF), 2 * D_HALF)
            s1 = dot(keys_ref[2 * h], qt_ref[pl.ds(base, D_HALF), :].astype(BF16))
            s2 = dot(keys_ref[2 * h + 1], qt_ref[pl.ds(base + D_HALF, D_HALF), :].astype(BF16))

            def store(tables):
                for ref, table in zip((n1_ref, w1_ref, rank2_ref, e2_ref), tables):
                    ref[h] = table

            a, _, used1 = _top_distinct(s1, False)
            b, _, used2 = _top_distinct(s2, False)
            repeats = jnp.maximum(jnp.max(used1), jnp.max(used2)) > TOP_K

            @pl.when(jnp.logical_not(repeats))
            def _all_distinct():
                store(_peer_tables(s1, s2, a, None, b, None))

            @pl.when(repeats)
            def _with_multiplicities():
                a_c, na, _ = _top_distinct(s1, True)
                b_c, nb, _ = _top_distinct(s2, True)
                store(_peer_tables(s1, s2, a_c, na, b_c, nb))

            return carry

        lax.fori_loop(0, PEER_HEADS, route_head, 0)
        hid_a[...] = dot(u0_ref[...], xt)
        p_b[...] = jnp.zeros_like(p_b)
        acc_ref[...] = jnp.zeros_like(acc_ref)

    @pl.when(j < PEER_STEPS)
    def _step():
        xt = xt_ref[...]
        tables = (n1_ref, w1_ref, rank2_ref, e2_ref)
        _peer_gated_act(p_a, hid_a, 2 * j * PEER_SUBKEYS, *tables)
        hid_b[...] = dot(ua_ref[...], xt)
        acc_ref[...] += dot(va_ref[...], _packed_bf16(p_b))
        _peer_gated_act(p_b, hid_b, (2 * j + 1) * PEER_SUBKEYS, *tables)
        hid_a[...] = dot(ub_ref[...], xt)
        acc_ref[...] += dot(vb_ref[...], _packed_bf16(p_a))

    @pl.when(j == PEER_STEPS)
    def _finish():
        ff = acc_ref[...] + dot(va_ref[...], _packed_bf16(p_b))
        y_ref[...] = _layer_norm(ALPHA * h_ref[...] + ff.T, g2_ref[...], b2_ref[...])


def _peer(h, wq_t, keys, u, v_t, g2, b2):
    t = h.shape[0]
    last = PEER_BLOCKS - 1
    tok = lambda i, j: (i, 0)
    fixed2 = lambda i, j: (0, 0)
    u_spec = lambda index: pl.BlockSpec((PEER_BLOCK, D_MODEL), index)
    v_spec = lambda index: pl.BlockSpec((D_MODEL, PEER_BLOCK), index)
    head_tile = (PEER_HEADS, N_KEYS, PEER_TOKENS)
    return pl.pallas_call(
        _peer_kernel,
        grid=(t // PEER_TOKENS, PEER_STEPS + 1),
        in_specs=[pl.BlockSpec((PEER_TOKENS, D_MODEL), tok),
                  pl.BlockSpec(wq_t.shape, fixed2),
                  pl.BlockSpec(keys.shape, lambda i, j: (0, 0, 0)),
                  u_spec(fixed2),
                  u_spec(lambda i, j: (jnp.minimum(2 * j + 1, last), 0)),
                  u_spec(lambda i, j: (jnp.minimum(2 * j + 2, last), 0)),
                  v_spec(lambda i, j: (0, jnp.maximum(2 * j - 1, 0))),
                  v_spec(lambda i, j: (0, jnp.minimum(2 * j, last))),
                  pl.BlockSpec((1, D_MODEL), fixed2),
                  pl.BlockSpec((1, D_MODEL), fixed2)],
        out_specs=pl.BlockSpec((PEER_TOKENS, D_MODEL), tok),
        out_shape=jax.ShapeDtypeStruct((t, D_MODEL), F32),
        scratch_shapes=[pltpu.VMEM((D_MODEL, PEER_TOKENS), BF16),
                        pltpu.VMEM((PEER_HEADS * 2 * D_HALF, PEER_TOKENS), F32),
                        pltpu.VMEM(head_tile, F32), pltpu.VMEM(head_tile, F32),
                        pltpu.VMEM(head_tile, BF16), pltpu.VMEM(head_tile, BF16),
                        pltpu.VMEM((PEER_BLOCK, PEER_TOKENS), F32),
                        pltpu.VMEM((PEER_BLOCK, PEER_TOKENS), F32),
                        pltpu.VMEM((PEER_BLOCK // 2, PEER_TOKENS), jnp.uint32),
                        pltpu.VMEM((PEER_BLOCK // 2, PEER_TOKENS), jnp.uint32),
                        pltpu.VMEM((D_MODEL, PEER_TOKENS), F32)],
        compiler_params=pltpu.CompilerParams(dimension_semantics=("parallel", "arbitrary"),
                                             vmem_limit_bytes=VMEM_LIMIT),
        name="peer",
    )(h, wq_t, keys, u, u, u, v_t, v_t, g2, b2)


def kernel(x_prompt, x_sample, state_dn_conv, state_dn_rec, state_ssm_conv, state_ssm_rec, meta_tokens, ln_in_g, ln_in_b, w_in, dn_conv_w, dn_A_log, dn_dt_bias, dn_norm_g, ssm_conv_w, ssm_conv_b, ssm_A_log, ssm_dt_bias, ssm_D, ssm_norm_g, w_out, ln1_g, ln1_b, peer_w_q, peer_keys, peer_u, peer_v, ln2_g, ln2_b):
    n_prompt, seq, _ = x_prompt.shape
    n_sample, dec_seq, _ = x_sample.shape
    row = lambda v: v.reshape(1, -1)

    w = w_in[0]
    o_z = DN_QKV
    o_b = o_z + DN_QK
    o_a = o_b + DN_HEADS
    o_sz = o_a + DN_HEADS
    o_xbc = o_sz + D_SSM
    o_dt = o_xbc + SSM_XBC
    gates = jnp.concatenate(
        [w[:, o_b:o_a], w[:, o_a:o_sz], w[:, o_dt:],
         jnp.zeros((D_MODEL, GATE_LANES - 2 * DN_HEADS - SSM_HEADS), w.dtype)], axis=1)
    w_proj = jnp.concatenate(
        [w[:, :o_z], w[:, o_z:o_b], w[:, o_sz:o_xbc], w[:, o_xbc:o_dt], gates],
        axis=1).astype(BF16)
    w_o = w_out[0].astype(BF16)
    wq_t = peer_w_q[0].T.astype(BF16)
    keys = peer_keys[0].reshape(PEER_HEADS * 2, N_KEYS, D_HALF).astype(BF16)
    u_tab = peer_u[0].astype(BF16)
    v_t = peer_v[0].T.astype(BF16)

    g_in, b_in = row(ln_in_g), row(ln_in_b)
    mixer_w = (dn_conv_w[0], row(dn_A_log[0]), row(dn_dt_bias[0]), row(dn_norm_g[0]),
               ssm_conv_w[0], row(ssm_conv_b[0]), row(ssm_A_log[0]), row(ssm_dt_bias[0]),
               row(ssm_D[0]), row(ssm_norm_g[0]))

    def trunk(x, init, n_seq, length, chunk, shared_init):
        proj = _ln_proj(x, g_in, b_in, w_proj, tm=min(256, x.shape[0]))
        om, states = _mixer(proj, init, mixer_w, n_seq, length, chunk, shared_init)
        return om, states

    zero_init = (jnp.zeros((1, CONV_K - 1, DN_QKV), F32),
                 jnp.zeros((1, DN_HEADS, DN_HEAD, DN_HEAD), F32),
                 jnp.zeros((1, CONV_K - 1, SSM_XBC), F32),
                 jnp.zeros((1, SSM_HEADS, SSM_HEAD, SSM_STATE), F32))
    _, meta_state = trunk(meta_tokens, zero_init, 1, N_META, N_META, False)

    xp = x_prompt.reshape(n_prompt * seq, D_MODEL)
    om_p, st_p = trunk(xp, meta_state, n_prompt, seq, CHUNK, True)
    xs = x_sample.reshape(n_sample * dec_seq, D_MODEL)
    init_s = (state_dn_conv[0], state_dn_rec[0], state_ssm_conv[0], state_ssm_rec[0])
    om_s, st_s = trunk(xs, init_s, n_sample, dec_seq, dec_seq, False)

    def tail(x, om):
        h1 = _out_ln1(x, om, w_o, g_in, b_in, row(ln1_g[0]), row(ln1_b[0]), tm=256)
        return _peer(h1, wq_t, keys, u_tab, v_t, row(ln2_g[0]), row(ln2_b[0]))

    y_p = tail(xp, om_p).reshape(n_prompt, seq, D_MODEL)
    y_s = tail(xs, om_s).reshape(n_sample, dec_seq, D_MODEL)
    return (y_p, y_s) + tuple(s[None] for s in st_p) + tuple(s[None] for s in st_s)
```

```python
import functools
import math

import numpy as np
import jax
import jax.numpy as jnp
from jax import lax
from jax.experimental import pallas as pl
from jax.experimental.pallas import tpu as pltpu

F32 = jnp.float32
BF16 = jnp.bfloat16

D_MODEL = 1024
N_META = 16
DN_HEADS = 16
DN_HEAD = 64
DN_QK = DN_HEADS * DN_HEAD
DN_QKV = 3 * DN_QK
SSM_HEADS = 16
SSM_HEAD = 64
SSM_GROUPS = 2
SSM_STATE = 128
D_SSM = SSM_HEADS * SSM_HEAD
SSM_BC = SSM_GROUPS * SSM_STATE
SSM_XBC = D_SSM + 2 * SSM_BC
HEADS_PER_GROUP = SSM_HEADS // SSM_GROUPS
D_MIX = DN_QK + D_SSM
CONV_K = 4
CHUNK = 64
PEER_HEADS = 8
N_KEYS = 128
D_HALF = 128
TOP_K = 16
ALPHA = 2.0 ** 0.25
GATE_LANES = 128
LANES = 128
MXU_DIM = 256
VMEM_LIMIT = 62 * 1024 * 1024


def _layer_norm(x, g, b, eps=1e-5):
    mu = jnp.mean(x, axis=-1, keepdims=True)
    xc = x - mu
    var = jnp.mean(xc * xc, axis=-1, keepdims=True)
    return xc * lax.rsqrt(var + eps) * g + b


def _mm(a, b):
    return jnp.dot(a.astype(BF16), b.astype(BF16), preferred_element_type=F32)


def _mm_nt(a, b):
    return lax.dot_general(a.astype(BF16), b.astype(BF16), (((1,), (1,)), ((), ())),
                           preferred_element_type=F32)


def _mm_tn(a, b):
    return lax.dot_general(a.astype(BF16), b.astype(BF16), (((0,), (0,)), ((), ())),
                           preferred_element_type=F32)


def _split(x):
    hi = x.astype(BF16)
    return hi, (x - hi.astype(F32)).astype(BF16)


def _mm2_split(a, b):
    dot = functools.partial(jnp.dot, preferred_element_type=F32)
    return dot(a[0], b[0]) + dot(a[0], b[1])


def _mm_01(m01, x):
    hi = x.astype(BF16)
    r = x - hi.astype(F32)
    mid = r.astype(BF16)
    lo = (r - mid.astype(F32)).astype(BF16)
    dot = functools.partial(jnp.dot, preferred_element_type=F32)
    return dot(m01, hi) + dot(m01, mid) + dot(m01, lo)


def _softplus(x):
    return jnp.maximum(x, 0.0) + jnp.log1p(jnp.exp(-jnp.abs(x)))


PROJ_SPLITS = (DN_QKV, DN_QK, D_SSM, SSM_XBC, GATE_LANES)
PROJ_COLS = sum(PROJ_SPLITS)


def _ln_proj_kernel(x_ref, g_ref, b_ref, w_ref, qkv_ref, z_ref, sz_ref, xbc_ref, gate_ref):
    xn = _layer_norm(x_ref[...], g_ref[...], b_ref[...]).astype(BF16)
    start = 0
    for out_ref, width in zip((qkv_ref, z_ref, sz_ref, xbc_ref, gate_ref), PROJ_SPLITS):
        out_ref[...] = jnp.dot(xn, w_ref[:, start:start + width], preferred_element_type=F32)
        start += width


def _ln_proj(x, g, b, w, tm):
    t = x.shape[0]
    row = lambda i: (i, 0)
    fixed = lambda i: (0, 0)
    return pl.pallas_call(
        _ln_proj_kernel,
        grid=(t // tm,),
        in_specs=[pl.BlockSpec((tm, D_MODEL), row),
                  pl.BlockSpec((1, D_MODEL), fixed),
                  pl.BlockSpec((1, D_MODEL), fixed),
                  pl.BlockSpec((D_MODEL, PROJ_COLS), fixed, pipeline_mode=pl.Buffered(1))],
        out_specs=[pl.BlockSpec((tm, width), row) for width in PROJ_SPLITS],
        out_shape=[jax.ShapeDtypeStruct((t, width), F32) for width in PROJ_SPLITS],
        compiler_params=pltpu.CompilerParams(dimension_semantics=("parallel",),
                                             vmem_limit_bytes=VMEM_LIMIT),
        name="ln_proj",
    )(x, g, b, w)


CONV_PAD = 8
SOLVE_SPLIT_LEVELS = 3


def _causal_conv(buf_ref, x, w, chunk):
    buf_ref[CONV_PAD:CONV_PAD + chunk, :] = x
    first = CONV_PAD - (CONV_K - 1)
    y = w[0:1] * buf_ref[first:first + chunk, :]
    for i in range(1, CONV_K):
        y = y + w[i:i + 1] * buf_ref[first + i:first + i + chunk, :]
    tail = buf_ref[first + chunk:CONV_PAD + chunk, :]
    buf_ref[first:CONV_PAD, :] = tail
    return y, tail


def _stack_lanes(x, first, count, width):
    return jnp.concatenate(
        [x[:, (first + i) * width:(first + i + 1) * width] for i in range(count)], axis=0)


def _unstack_lanes(xs, count, chunk):
    return jnp.concatenate([xs[i * chunk:(i + 1) * chunk, :] for i in range(count)], axis=1)


def _stack_rows(xt, first, count):
    return jnp.concatenate([xt[first + i:first + i + 1, :] for i in range(count)], axis=1)


def _repeat_rows(v, first, count, reps):
    return jnp.concatenate(
        [jnp.broadcast_to(v[:, first + i:first + i + 1], (reps, 1)) for i in range(count)], axis=0)


def _spread_heads(a, first):
    return jnp.concatenate(
        [jnp.broadcast_to(a[:, first + r:first + r + 1], (a.shape[0], SSM_HEAD))
         for r in range(HEADS_PER_GROUP)], axis=1)


def _block_diag(x, count, own_block):
    return jnp.where(own_block, jnp.concatenate([x] * count, axis=1), 0.0)


def _transpose_rows(x, chunk):
    if chunk < LANES:
        x = jnp.concatenate([x, jnp.zeros((LANES - chunk, LANES), F32)], axis=0)
    return x.T[:, :chunk]


def _mixer_kernel(qkv_ref, z_ref, sz_ref, xbc_ref, gate_ref,
                  dnc0_ref, dnr0_ref, ssc0_ref, ssr0_ref,
                  dncw_ref, dna_ref, dndt_ref, dng_ref,
                  sscw_ref, sscb_ref, ssa_ref, ssdt_ref, ssd_ref, ssg_ref,
                  om_ref, dnc_ref, dnr_ref, ssc_ref, ssr_ref,
                  dn_buf, ss_buf, ht_ref, *, chunk, n_chunks):
    c = pl.program_id(1)

    @pl.when(c == 0)
    def _init():
        dn_buf[CONV_PAD - (CONV_K - 1):CONV_PAD, :] = dnc0_ref[0]
        ss_buf[CONV_PAD - (CONV_K - 1):CONV_PAD, :] = ssc0_ref[0]
        dnr_ref[0] = dnr0_ref[0]
        for g in range(SSM_GROUPS):
            rows = jnp.concatenate(
                [ssr0_ref[0, g * HEADS_PER_GROUP + r] for r in range(HEADS_PER_GROUP)], axis=0)
            ht_ref[g] = rows.T

    stack = min(DN_HEADS, MXU_DIM // chunk)
    ss_stack = min(stack, HEADS_PER_GROUP)
    shift = int(math.log2(chunk))
    n_squarings = shift - 1

    def block_masks(n):
        r_i = lax.broadcasted_iota(jnp.int32, (n, n), 0)
        c_i = lax.broadcasted_iota(jnp.int32, (n, n), 1)
        same = (r_i >> shift) == (c_i >> shift)
        return same & (r_i >= c_i), same & (r_i > c_i)

    causal, strict = block_masks(stack * chunk)
    causal_ss = causal if ss_stack == stack else block_masks(ss_stack * chunk)[0]
    head_of_row = lax.broadcasted_iota(jnp.int32, (stack * chunk, stack * DN_HEAD), 0) >> shift
    head_of_lane = (lax.broadcasted_iota(jnp.int32, (stack * chunk, stack * DN_HEAD), 1)
                    >> int(math.log2(DN_HEAD)))
    own_head = head_of_row == head_of_lane
    tril01 = (lax.broadcasted_iota(jnp.int32, (chunk, chunk), 0)
              >= lax.broadcasted_iota(jnp.int32, (chunk, chunk), 1)).astype(BF16)

    gate = gate_ref[...]
    beta = jax.nn.sigmoid(gate[:, 0:DN_HEADS])
    g_dn = -jnp.exp(dna_ref[...]) * _softplus(gate[:, DN_HEADS:2 * DN_HEADS] + dndt_ref[...])
    dt = _softplus(gate[:, 2 * DN_HEADS:2 * DN_HEADS + SSM_HEADS] + ssdt_ref[...])
    da = dt * (-jnp.exp(ssa_ref[...]))
    logdec = jnp.concatenate(
        [g_dn, da, jnp.zeros((chunk, LANES - DN_HEADS - SSM_HEADS), F32)], axis=1)
    cum = _mm_01(tril01, logdec)
    cum_t = _transpose_rows(cum, chunk)
    cum_last = cum[chunk - 1:chunk, :]

    conv, dn_tail = _causal_conv(dn_buf, qkv_ref[...], dncw_ref[...], chunk)
    qkv = jax.nn.silu(conv)
    dn_groups = []
    for g in range(DN_HEADS // stack):
        h0 = g * stack
        qs = _stack_lanes(qkv, h0, stack, DN_HEAD)
        ks = _stack_lanes(qkv, DN_HEADS + h0, stack, DN_HEAD)
        vs = _stack_lanes(qkv, 2 * DN_HEADS + h0, stack, DN_HEAD)
        qn = qs * lax.rsqrt(jnp.sum(qs * qs, axis=-1, keepdims=True) + 1e-6) * (DN_HEAD ** -0.5)
        kn = ks * lax.rsqrt(jnp.sum(ks * ks, axis=-1, keepdims=True) + 1e-6)
        cc = _stack_lanes(cum, h0, stack, 1)
        bcol = _stack_lanes(beta, h0, stack, 1)
        decay = jnp.exp(jnp.where(causal, cc - _stack_rows(cum_t, h0, stack), -jnp.inf))
        ecc = jnp.exp(cc)
        a_mat = jnp.where(strict, bcol * _mm_nt(kn, kn) * decay, 0.0)
        rhs = jnp.concatenate([bcol * vs, (bcol * ecc) * kn], axis=1)
        qk = jnp.where(causal, _mm_nt(qn, kn) * decay, 0.0)
        kd = jnp.exp(_repeat_rows(cum_last, h0, stack, chunk) - cc) * kn
        dn_groups.append((qn, ecc, a_mat, rhs, qk, kd))

    pows = [_split(grp[2]) for grp in dn_groups]
    sols = [grp[3] - _mm2_split(p, _split(grp[3])) for p, grp in zip(pows, dn_groups)]
    for level in range(n_squarings):
        if level < SOLVE_SPLIT_LEVELS:
            pows = [_split(_mm2_split(p, p)) for p in pows]
            sols = [s + _mm2_split(p, _split(s)) for p, s in zip(pows, sols)]
        else:
            pows = [(jnp.dot(p[0], p[0], preferred_element_type=F32).astype(BF16),) for p in pows]
            sols = [s + jnp.dot(p[0], s.astype(BF16), preferred_element_type=F32)
                    for p, s in zip(pows, sols)]

    conv, ss_tail = _causal_conv(ss_buf, xbc_ref[...], sscw_ref[...], chunk)
    xbc = jax.nn.silu(conv + sscb_ref[...])
    cum_ss = cum[:, DN_HEADS:DN_HEADS + SSM_HEADS]
    end_ss = cum_last[:, DN_HEADS:DN_HEADS + SSM_HEADS]
    ecum_ss = jnp.exp(cum_ss)
    tok_w = dt * jnp.exp(end_ss - cum_ss)
    e_end_ss = jnp.exp(end_ss)
    d_skip = ssd_ref[...]
    y_groups = []
    for g in range(SSM_GROUPS):
        g0 = g * HEADS_PER_GROUP
        b_g = xbc[:, D_SSM + g * SSM_STATE:D_SSM + (g + 1) * SSM_STATE]
        c_g = xbc[:, D_SSM + SSM_BC + g * SSM_STATE:D_SSM + SSM_BC + (g + 1) * SSM_STATE]
        cb = _mm_nt(c_g, b_g)
        cb_wide = jnp.concatenate([cb] * ss_stack, axis=1)
        cb_tiled = jnp.concatenate([cb_wide] * ss_stack, axis=0)
        intra = []
        for h0 in range(g0, g0 + HEADS_PER_GROUP, ss_stack):
            cc = _stack_lanes(cum_ss, h0, ss_stack, 1)
            lmat = jnp.exp(jnp.where(causal_ss, cc - _stack_rows(cum_t, DN_HEADS + h0, ss_stack),
                                     -jnp.inf))
            xdt = _stack_lanes(xbc, h0, ss_stack, SSM_HEAD) * _stack_lanes(dt, h0, ss_stack, 1)
            intra.append(_unstack_lanes(_mm(lmat * cb_tiled, xdt), ss_stack, chunk))
        x_g = xbc[:, g0 * SSM_HEAD:(g0 + HEADS_PER_GROUP) * SSM_HEAD]
        ht = ht_ref[g]
        y_g = (jnp.concatenate(intra, axis=1)
               + _spread_heads(ecum_ss, g0) * _mm(c_g, ht)
               + _spread_heads(d_skip, g0) * x_g)
        y_groups.append(y_g)
        xw = x_g * _spread_heads(tok_w, g0)
        ht_ref[g] = _spread_heads(e_end_ss, g0) * ht + _mm_tn(b_g, xw)
    y = jnp.concatenate(y_groups, axis=1) * jax.nn.silu(sz_ref[...])
    y = y * lax.rsqrt(jnp.mean(y * y, axis=-1, keepdims=True) + 1e-6) * ssg_ref[...]
    om_ref[:, DN_QK:D_MIX] = y

    dn_gain = dng_ref[...]
    e_end_dn = jnp.exp(cum_last)
    o_groups = []
    n_rows = stack * chunk
    for g, (qn, ecc, _, _, qk, kd) in enumerate(dn_groups):
        h0 = g * stack
        s = jnp.concatenate([dnr_ref[0, h0 + i] for i in range(stack)], axis=0)
        u = sols[g][:, :DN_HEAD]
        w = sols[g][:, DN_HEAD:]
        wq = _mm(jnp.concatenate([_block_diag(w, stack, own_head),
                                  _block_diag(qn, stack, own_head)], axis=0), s)
        delta = u - wq[:n_rows]
        o = ecc * wq[n_rows:] + _mm(qk, delta)
        s_new = (_repeat_rows(e_end_dn, h0, stack, DN_HEAD) * s
                 + _mm_tn(_block_diag(kd, stack, own_head), delta))
        for i in range(stack):
            dnr_ref[0, h0 + i] = s_new[i * DN_HEAD:(i + 1) * DN_HEAD, :]
        on = o * lax.rsqrt(jnp.mean(o * o, axis=-1, keepdims=True) + 1e-6) * dn_gain
        o_groups.append(_unstack_lanes(on, stack, chunk))
    om_ref[:, 0:DN_QK] = jnp.concatenate(o_groups, axis=1) * jax.nn.silu(z_ref[...])

    @pl.when(c == n_chunks - 1)
    def _finish():
        dnc_ref[0] = dn_tail
        ssc_ref[0] = ss_tail
        for g in range(SSM_GROUPS):
            rows = ht_ref[g].T
            for r in range(HEADS_PER_GROUP):
                ssr_ref[0, g * HEADS_PER_GROUP + r] = rows[r * SSM_HEAD:(r + 1) * SSM_HEAD, :]


def _mixer(proj, init, weights, n_seq, length, chunk, shared_init):
    n_chunks = length // chunk
    tok = lambda b, c: (b * n_chunks + c, 0)
    if shared_init:
        st3 = lambda b, c: (0, 0, 0)
        st4 = lambda b, c: (0, 0, 0, 0)
    else:
        st3 = lambda b, c: (b, 0, 0)
        st4 = lambda b, c: (b, 0, 0, 0)
    out3 = lambda b, c: (b, 0, 0)
    out4 = lambda b, c: (b, 0, 0, 0)
    fixed = lambda b, c: (0, 0)
    state_shapes = ((CONV_K - 1, DN_QKV), (DN_HEADS, DN_HEAD, DN_HEAD),
                    (CONV_K - 1, SSM_XBC), (SSM_HEADS, SSM_HEAD, SSM_STATE))

    def state_spec(shape, m3, m4):
        return pl.BlockSpec((1,) + shape, m3 if len(shape) == 2 else m4)

    in_specs = [pl.BlockSpec((chunk, width), tok) for width in PROJ_SPLITS]
    in_specs += [state_spec(s, st3, st4) for s in state_shapes]
    in_specs += [pl.BlockSpec(w.shape, fixed) for w in weights]
    out_specs = [pl.BlockSpec((chunk, D_MIX), tok)]
    out_specs += [state_spec(s, out3, out4) for s in state_shapes]
    out_shape = [jax.ShapeDtypeStruct((n_seq * length, D_MIX), F32)]
    out_shape += [jax.ShapeDtypeStruct((n_seq,) + s, F32) for s in state_shapes]
    outs = pl.pallas_call(
        functools.partial(_mixer_kernel, chunk=chunk, n_chunks=n_chunks),
        grid=(n_seq, n_chunks),
        in_specs=in_specs,
        out_specs=out_specs,
        out_shape=out_shape,
        scratch_shapes=[pltpu.VMEM((CONV_PAD + chunk, DN_QKV), F32),
                        pltpu.VMEM((CONV_PAD + chunk, SSM_XBC), F32),
                        pltpu.VMEM((SSM_GROUPS, SSM_STATE, HEADS_PER_GROUP * SSM_HEAD), F32)],
        compiler_params=pltpu.CompilerParams(dimension_semantics=("parallel", "arbitrary"),
                                             vmem_limit_bytes=VMEM_LIMIT),
        name=f"mixer_c{chunk}",
    )(*proj, *init, *weights)
    return outs[0], tuple(outs[1:])


def _out_ln1_kernel(x_ref, om_ref, w_ref, g0_ref, b0_ref, g1_ref, b1_ref, h_ref):
    h0 = _layer_norm(x_ref[...], g0_ref[...], b0_ref[...])
    mix = jnp.dot(om_ref[...].astype(BF16), w_ref[...], preferred_element_type=F32)
    h_ref[...] = _layer_norm(ALPHA * h0 + mix, g1_ref[...], b1_ref[...])


def _out_ln1(x, om, w, g0, b0, g1, b1, tm):
    t = x.shape[0]
    row = lambda i: (i, 0)
    fixed = lambda i: (0, 0)
    vec = pl.BlockSpec((1, D_MODEL), fixed)
    return pl.pallas_call(
        _out_ln1_kernel,
        grid=(t // tm,),
        in_specs=[pl.BlockSpec((tm, D_MODEL), row), pl.BlockSpec((tm, D_MIX), row),
                  pl.BlockSpec((D_MIX, D_MODEL), fixed), vec, vec, vec, vec],
        out_specs=pl.BlockSpec((tm, D_MODEL), row),
        out_shape=jax.ShapeDtypeStruct((t, D_MODEL), F32),
        compiler_params=pltpu.CompilerParams(dimension_semantics=("parallel",),
                                             vmem_limit_bytes=VMEM_LIMIT),
        name="out_ln1",
    )(x, om, w, g0, b0, g1, b1)


PEER_TOKENS = 512
PEER_SUBKEYS = 8
PEER_BLOCK = PEER_SUBKEYS * N_KEYS
PEER_BLOCKS = N_KEYS // PEER_SUBKEYS
PEER_STEPS = PEER_BLOCKS // 2
PAIR_LIMITS = tuple(TOP_K // (j + 1) for j in range(TOP_K))


def _top_distinct(s, counted):
    vals, counts = [], []
    for _ in range(TOP_K):
        m = jnp.max(s, axis=0, keepdims=True)
        eq = s == m
        if counted:
            counts.append(jnp.sum(jnp.where(eq, 1.0, 0.0), axis=0, keepdims=True))
        vals.append(m)
        s = jnp.where(eq, -jnp.inf, s)
    used = jnp.sum(jnp.where(s == -jnp.inf, 1.0, 0.0), axis=0, keepdims=True)
    return vals, counts, used


def _peer_tables(s1, s2, a, na, b, nb):
    cand, mult, gate = [], [], []
    for j in range(TOP_K):
        ea = jnp.exp(a[j] - a[0])
        for l in range(PAIR_LIMITS[j]):
            cand.append(a[j] + b[l])
            mult.append(jnp.ones_like(a[0]) if na is None else na[j] * nb[l])
            gate.append(ea * jnp.exp(b[l] - b[0]))
    cand = jnp.concatenate(cand, axis=0)
    mult = jnp.concatenate(mult, axis=0)
    gate = jnp.concatenate(gate, axis=0)
    above = jnp.zeros_like(cand)
    for i in range(cand.shape[0]):
        above = above + jnp.where(cand[i:i + 1, :] > cand, mult[i:i + 1, :], 0.0)
    theta = jnp.min(jnp.where(above < TOP_K, cand, jnp.inf), axis=0, keepdims=True)
    zsum = jnp.sum(jnp.where(cand >= theta, mult * gate, 0.0), axis=0, keepdims=True)
    n1 = jnp.where(s1 + b[TOP_K - 1] >= theta, 1.0, 0.0)
    passed = []
    for step in (8, 4, 2, 1):
        probes = [b[i + step - 1] for i in range(0, TOP_K, 2 * step)]
        for ok in reversed(passed):
            probes = [jnp.where(ok, probes[i + 1], probes[i]) for i in range(0, len(probes), 2)]
        ok = s1 + probes[0] >= theta
        passed.append(ok)
        n1 = n1 + jnp.where(ok, float(step), 0.0)
    rank2 = jnp.zeros_like(s2)
    for l in range(TOP_K):
        rank2 = rank2 + jnp.where(b[l] > s2, 1.0, 0.0)
    w1 = jnp.exp(s1 - a[0]) / zsum
    e2 = jnp.exp(s2 - b[0])
    return n1, w1, rank2.astype(BF16), e2.astype(BF16)


def _packed_bf16(words_ref):
    return pltpu.bitcast(words_ref[...], BF16)


def _peer_gated_act(p_ref, hid_ref, first_key, n1_ref, w1_ref, rank2_ref, e2_ref):
    sqrt_half = np.float32(np.sqrt(0.5))
    tile = (N_KEYS, LANES)
    for k in range(PEER_SUBKEYS):
        i1 = first_key + k
        rows = slice(k * N_KEYS, (k + 1) * N_KEYS)
        n_rows = [n1_ref[h, pl.ds(i1, 1), :] for h in range(PEER_HEADS)]
        w_rows = [w1_ref[h, pl.ds(i1, 1), :] for h in range(PEER_HEADS)]
        for s in range(PEER_TOKENS // LANES):
            lanes = slice(s * LANES, (s + 1) * LANES)
            gate = jnp.zeros(tile, BF16)
            for h in range(PEER_HEADS):
                n_row = jnp.broadcast_to(n_rows[h][:, lanes], tile).astype(BF16)
                w_row = jnp.broadcast_to(w_rows[h][:, lanes], tile).astype(BF16)
                e2 = e2_ref[h, :, lanes]
                sel = jnp.where(rank2_ref[h, :, lanes] < n_row, e2, jnp.zeros_like(e2))
                gate = gate + sel * w_row
            hk = hid_ref[rows, lanes]
            act = 0.5 * hk * (1.0 + lax.erf(hk * sqrt_half))
            words = slice(k * (N_KEYS // 2), (k + 1) * (N_KEYS // 2))
            p_ref[words, lanes] = pltpu.bitcast(gate * act.astype(BF16), jnp.uint32)


def _peer_kernel(h_ref, wq_ref, keys_ref, u0_ref, ua_ref, ub_ref, va_ref, vb_ref, g2_ref, b2_ref,
                 y_ref, xt_ref, qt_ref, n1_ref, w1_ref, rank2_ref, e2_ref,
                 hid_a, hid_b, p_a, p_b, acc_ref):
    j = pl.program_id(1)
    dot = functools.partial(jnp.dot, preferred_element_type=F32)

    @pl.when(j == 0)
    def _route():
        xt = h_ref[...].T.astype(BF16)
        xt_ref[...] = xt
        qt_ref[...] = dot(wq_ref[...], xt)

        def route_head(h, carry):
            base = pl.multiple_of(h * (2 * D_HALF), 2 * D_HALF)
            s1 = dot(keys_ref[2 * h], qt_ref[pl.ds(base, D_HALF), :].astype(BF16))
            s2 = dot(keys_ref[2 * h + 1], qt_ref[pl.ds(base + D_HALF, D_HALF), :].astype(BF16))

            def store(tables):
                for ref, table in zip((n1_ref, w1_ref, rank2_ref, e2_ref), tables):
                    ref[h] = table

            a, _, used1 = _top_distinct(s1, False)
            b, _, used2 = _top_distinct(s2, False)
            repeats = jnp.maximum(jnp.max(used1), jnp.max(used2)) > TOP_K

            @pl.when(jnp.logical_not(repeats))
            def _all_distinct():
                store(_peer_tables(s1, s2, a, None, b, None))

            @pl.when(repeats)
            def _with_multiplicities():
                a_c, na, _ = _top_distinct(s1, True)
                b_c, nb, _ = _top_distinct(s2, True)
                store(_peer_tables(s1, s2, a_c, na, b_c, nb))

            return carry

        lax.fori_loop(0, PEER_HEADS, route_head, 0)
        hid_a[...] = dot(u0_ref[...], xt)
        p_b[...] = jnp.zeros_like(p_b)
        acc_ref[...] = jnp.zeros_like(acc_ref)

    @pl.when(j < PEER_STEPS)
    def _step():
        xt = xt_ref[...]
        tables = (n1_ref, w1_ref, rank2_ref, e2_ref)
        _peer_gated_act(p_a, hid_a, 2 * j * PEER_SUBKEYS, *tables)
        hid_b[...] = dot(ua_ref[...], xt)
        acc_ref[...] += dot(va_ref[...], _packed_bf16(p_b))
        _peer_gated_act(p_b, hid_b, (2 * j + 1) * PEER_SUBKEYS, *tables)
        hid_a[...] = dot(ub_ref[...], xt)
        acc_ref[...] += dot(vb_ref[...], _packed_bf16(p_a))

    @pl.when(j == PEER_STEPS)
    def _finish():
        ff = acc_ref[...] + dot(va_ref[...], _packed_bf16(p_b))
        y_ref[...] = _layer_norm(ALPHA * h_ref[...] + ff.T, g2_ref[...], b2_ref[...])


def _peer(h, wq_t, keys, u, v_t, g2, b2):
    t = h.shape[0]
    last = PEER_BLOCKS - 1
    tok = lambda i, j: (i, 0)
    fixed2 = lambda i, j: (0, 0)
    u_spec = lambda index: pl.BlockSpec((PEER_BLOCK, D_MODEL), index)
    v_spec = lambda index: pl.BlockSpec((D_MODEL, PEER_BLOCK), index)
    head_tile = (PEER_HEADS, N_KEYS, PEER_TOKENS)
    return pl.pallas_call(
        _peer_kernel,
        grid=(t // PEER_TOKENS, PEER_STEPS + 1),
        in_specs=[pl.BlockSpec((PEER_TOKENS, D_MODEL), tok),
                  pl.BlockSpec(wq_t.shape, fixed2, pipeline_mode=pl.Buffered(1)),
                  pl.BlockSpec(keys.shape, lambda i, j: (0, 0, 0), pipeline_mode=pl.Buffered(1)),
                  pl.BlockSpec((PEER_BLOCK, D_MODEL), fixed2, pipeline_mode=pl.Buffered(1)),
                  u_spec(lambda i, j: (jnp.minimum(2 * j + 1, last), 0)),
                  u_spec(lambda i, j: (jnp.minimum(2 * j + 2, last), 0)),
                  v_spec(lambda i, j: (0, jnp.maximum(2 * j - 1, 0))),
                  v_spec(lambda i, j: (0, jnp.minimum(2 * j, last))),
                  pl.BlockSpec((1, D_MODEL), fixed2),
                  pl.BlockSpec((1, D_MODEL), fixed2)],
        out_specs=pl.BlockSpec((PEER_TOKENS, D_MODEL), tok),
        out_shape=jax.ShapeDtypeStruct((t, D_MODEL), F32),
        scratch_shapes=[pltpu.VMEM((D_MODEL, PEER_TOKENS), BF16),
                        pltpu.VMEM((PEER_HEADS * 2 * D_HALF, PEER_TOKENS), F32),
                        pltpu.VMEM(head_tile, F32), pltpu.VMEM(head_tile, F32),
                        pltpu.VMEM(head_tile, BF16), pltpu.VMEM(head_tile, BF16),
                        pltpu.VMEM((PEER_BLOCK, PEER_TOKENS), F32),
                        pltpu.VMEM((PEER_BLOCK, PEER_TOKENS), F32),
                        pltpu.VMEM((PEER_BLOCK // 2, PEER_TOKENS), jnp.uint32),
                        pltpu.VMEM((PEER_BLOCK // 2, PEER_TOKENS), jnp.uint32),
                        pltpu.VMEM((D_MODEL, PEER_TOKENS), F32)],
        compiler_params=pltpu.CompilerParams(dimension_semantics=("parallel", "arbitrary"),
                                             vmem_limit_bytes=VMEM_LIMIT),
        name="peer",
    )(h, wq_t, keys, u, u, u, v_t, v_t, g2, b2)


def kernel(x_prompt, x_sample, state_dn_conv, state_dn_rec, state_ssm_conv, state_ssm_rec, meta_tokens, ln_in_g, ln_in_b, w_in, dn_conv_w, dn_A_log, dn_dt_bias, dn_norm_g, ssm_conv_w, ssm_conv_b, ssm_A_log, ssm_dt_bias, ssm_D, ssm_norm_g, w_out, ln1_g, ln1_b, peer_w_q, peer_keys, peer_u, peer_v, ln2_g, ln2_b):
    n_prompt, seq, _ = x_prompt.shape
    n_sample, dec_seq, _ = x_sample.shape
    row = lambda v: v.reshape(1, -1)

    w = w_in[0]
    o_z = DN_QKV
    o_b = o_z + DN_QK
    o_a = o_b + DN_HEADS
    o_sz = o_a + DN_HEADS
    o_xbc = o_sz + D_SSM
    o_dt = o_xbc + SSM_XBC
    gates = jnp.concatenate(
        [w[:, o_b:o_a], w[:, o_a:o_sz], w[:, o_dt:],
         jnp.zeros((D_MODEL, GATE_LANES - 2 * DN_HEADS - SSM_HEADS), w.dtype)], axis=1)
    w_proj = jnp.concatenate(
        [w[:, :o_z], w[:, o_z:o_b], w[:, o_sz:o_xbc], w[:, o_xbc:o_dt], gates],
        axis=1).astype(BF16)
    w_o = w_out[0].astype(BF16)
    wq_t = peer_w_q[0].T.astype(BF16)
    keys = peer_keys[0].reshape(PEER_HEADS * 2, N_KEYS, D_HALF).astype(BF16)
    u_tab = peer_u[0].astype(BF16)
    v_t = peer_v[0].T.astype(BF16)

    g_in, b_in = row(ln_in_g), row(ln_in_b)
    mixer_w = (dn_conv_w[0], row(dn_A_log[0]), row(dn_dt_bias[0]), row(dn_norm_g[0]),
               ssm_conv_w[0], row(ssm_conv_b[0]), row(ssm_A_log[0]), row(ssm_dt_bias[0]),
               row(ssm_D[0]), row(ssm_norm_g[0]))

    def trunk(x, init, n_seq, length, chunk, shared_init):
        proj = _ln_proj(x, g_in, b_in, w_proj, tm=min(256, x.shape[0]))
        om, states = _mixer(proj, init, mixer_w, n_seq, length, chunk, shared_init)
        return om, states

    zero_init = (jnp.zeros((1, CONV_K - 1, DN_QKV), F32),
                 jnp.zeros((1, DN_HEADS, DN_HEAD, DN_HEAD), F32),
                 jnp.zeros((1, CONV_K - 1, SSM_XBC), F32),
                 jnp.zeros((1, SSM_HEADS, SSM_HEAD, SSM_STATE), F32))
    _, meta_state = trunk(meta_tokens, zero_init, 1, N_META, N_META, False)

    xp = x_prompt.reshape(n_prompt * seq, D_MODEL)
    om_p, st_p = trunk(xp, meta_state, n_prompt, seq, CHUNK, True)
    xs = x_sample.reshape(n_sample * dec_seq, D_MODEL)
    init_s = (state_dn_conv[0], state_dn_rec[0], state_ssm_conv[0], state_ssm_rec[0])
    om_s, st_s = trunk(xs, init_s, n_sample, dec_seq, dec_seq, False)

    def tail(x, om):
        h1 = _out_ln1(x, om, w_o, g_in, b_in, row(ln1_g[0]), row(ln1_b[0]), tm=256)
        return _peer(h1, wq_t, keys, u_tab, v_t, row(ln2_g[0]), row(ln2_b[0]))

    y_p = tail(xp, om_p).reshape(n_prompt, seq, D_MODEL)
    y_s = tail(xs, om_s).reshape(n_sample, dec_seq, D_MODEL)
    return (y_p, y_s) + tuple(s[None] for s in st_p) + tuple(s[None] for s in st_s)
```
